```python
import math
import jax
import jax.numpy as jnp
from jax import lax
import numpy as np

D_MODEL = 2048
BATCH = 4
SEQ = 2048
DEPTH = 2

D_MIX = D_MODEL
S5_WIDTH = D_MIX // 4
S5_GROUP = 16
S5_GROUPS = S5_WIDTH // S5_GROUP
S5_STATE = 64
LRU_WIDTH = (D_MIX - S5_WIDTH) // 2
LRU_HEADS = 8
LRU_HEAD_DIM = LRU_WIDTH // LRU_HEADS
CONV_WIDTH = 4
LRU_C = 8.0
GLA_WIDTH = D_MIX - S5_WIDTH - LRU_WIDTH
GLA_HEADS = 4
GLA_DV = GLA_WIDTH // GLA_HEADS
GLA_DK = GLA_DV // 2
GLA_GATE_RANK = 16
GLA_TAU = 16.0
GLA_CHUNK = 64
D_FF = 256 * ((8 * D_MODEL // 3 + 255) // 256)
N_EXPERTS = 8
TOP_K = 2
D_FF_EXPERT = 7 * D_MODEL // 2
N_ADA = 6
EPS = 1e-6
IN_SIZES = (S5_WIDTH, LRU_WIDTH, LRU_WIDTH, GLA_HEADS * GLA_DK, GLA_HEADS * GLA_DK, GLA_WIDTH, GLA_GATE_RANK, GLA_WIDTH)
D_IN = sum(IN_SIZES)
IN_SPLITS = tuple(int(v) for v in np.cumsum(IN_SIZES)[:-1])

kernel_name = "hybrid_s5_rglru_gla_moe_adaln"

F32 = jnp.float32


def rmsnorm(x, g):
    xf = x.astype(F32)
    y = xf * lax.rsqrt(jnp.mean(xf * xf, axis=-1, keepdims=True) + EPS)
    return (y * g.astype(F32)).astype(x.dtype)


def s5_mixer(u, a_re, a_im, log_dt, b_re, b_im, c_re, c_im, d_skip, w_glu, b_glu):
    bsz, s, _ = u.shape
    ar = a_re.astype(F32)
    ai = a_im.astype(F32)
    dt = jnp.exp(log_dt.astype(F32))[:, None]
    mag = jnp.exp(ar * dt)
    abar_re = mag * jnp.cos(ai * dt)
    abar_im = mag * jnp.sin(ai * dt)
    nr = abar_re - 1.0
    ni = abar_im
    den = ar * ar + ai * ai
    f_re = (nr * ar + ni * ai) / den
    f_im = (ni * ar - nr * ai) / den
    br = b_re.astype(F32)
    bi = b_im.astype(F32)
    bbar_re = f_re[..., None] * br - f_im[..., None] * bi
    bbar_im = f_re[..., None] * bi + f_im[..., None] * br
    ug = u.astype(F32).reshape(bsz, s, S5_GROUPS, S5_GROUP)
    bu_re = jnp.einsum('bsgc,gpc->bsgp', ug, bbar_re)
    bu_im = jnp.einsum('bsgc,gpc->bsgp', ug, bbar_im)
    a_re_b = jnp.broadcast_to(abar_re, bu_re.shape)
    a_im_b = jnp.broadcast_to(abar_im, bu_im.shape)

    def combine(left, right):
        a1r, a1i, b1r, b1i = left
        a2r, a2i, b2r, b2i = right
        return (a2r * a1r - a2i * a1i,
                a2r * a1i + a2i * a1r,
                a2r * b1r - a2i * b1i + b2r,
                a2r * b1i + a2i * b1r + b2i)

    _, _, h_re, h_im = lax.associative_scan(combine, (a_re_b, a_im_b, bu_re, bu_im), axis=1)
    y = (jnp.einsum('bsgp,gcp->bsgc', h_re, c_re.astype(F32))
         - jnp.einsum('bsgp,gcp->bsgc', h_im, c_im.astype(F32))
         + d_skip.astype(F32).reshape(S5_GROUPS, S5_GROUP) * ug)
    z = jax.nn.gelu(y.reshape(bsz, s, S5_WIDTH))
    zz = z @ w_glu.astype(F32) + b_glu.astype(F32)
    val, gate = jnp.split(zz, 2, axis=-1)
    return val * jax.nn.sigmoid(gate)


def rglru_mixer(xb, gb, conv_w, conv_b, w_a, b_a, w_x, b_x, lam):
    bsz, s, _ = xb.shape
    xf = xb.astype(F32)
    xc = lax.conv_general_dilated(
        xf, conv_w.astype(F32)[:, None, :], window_strides=(1,),
        padding=[(CONV_WIDTH - 1, 0)], dimension_numbers=('NWC', 'WIO', 'NWC'),
        feature_group_count=LRU_WIDTH) + conv_b.astype(F32)
    xh = xc.reshape(bsz, s, LRU_HEADS, LRU_HEAD_DIM)
    r = jax.nn.sigmoid(jnp.einsum('bshi,hij->bshj', xh, w_a.astype(F32))
                       + b_a.astype(F32).reshape(LRU_HEADS, LRU_HEAD_DIM))
    i = jax.nn.sigmoid(jnp.einsum('bshi,hij->bshj', xh, w_x.astype(F32))
                       + b_x.astype(F32).reshape(LRU_HEADS, LRU_HEAD_DIM))
    log_a = LRU_C * r * jax.nn.log_sigmoid(lam.astype(F32)).reshape(LRU_HEADS, LRU_HEAD_DIM)
    a = jnp.exp(log_a)
    mult = jnp.sqrt(jnp.maximum(-jnp.expm1(2.0 * log_a), 0.0))
    bt = mult * i * xh

    def combine(left, right):
        a1, b1 = left
        a2, b2 = right
        return a2 * a1, a2 * b1 + b2

    _, h = lax.associative_scan(combine, (a.reshape(bsz, s, LRU_WIDTH), bt.reshape(bsz, s, LRU_WIDTH)), axis=1)
    return h * jax.nn.gelu(gb.astype(F32))


def gla_mixer(q, k, v, g_low, og, w_g2, b_g2, head_g):
    bsz, s, _ = q.shape
    n = s // GLA_CHUNK

    def blocks(t, d):
        return t.astype(F32).reshape(bsz, n, GLA_CHUNK, GLA_HEADS, d).transpose(0, 3, 1, 2, 4)

    qb = blocks(q, GLA_DK) * (GLA_DK ** -0.5)
    kb = blocks(k, GLA_DK)
    vb = blocks(v, GLA_DV)
    gate_logit = g_low.astype(F32) @ w_g2.astype(F32) + b_g2.astype(F32)
    log_alpha = blocks(jax.nn.log_sigmoid(gate_logit) / GLA_TAU, GLA_DK)
    bcum = jnp.cumsum(log_alpha, axis=3)
    b_last = bcum[:, :, :, -1:, :]
    q_dec = qb * jnp.exp(bcum)
    k_in = kb * jnp.exp(-bcum)
    k_out = kb * jnp.exp(b_last - bcum)
    causal = jnp.tril(jnp.ones((GLA_CHUNK, GLA_CHUNK), dtype=bool))
    att = jnp.where(causal, jnp.einsum('bhnid,bhnjd->bhnij', q_dec, k_in), 0.0)
    o_intra = jnp.einsum('bhnij,bhnjv->bhniv', att, vb)
    delta = jnp.einsum('bhncd,bhncv->nbhdv', k_out, vb)
    gamma = jnp.moveaxis(jnp.exp(b_last[:, :, :, 0, :]), 2, 0)

    def step(state, inp):
        gam, dlt = inp
        return gam[..., None] * state + dlt, state

    state0 = jnp.zeros((bsz, GLA_HEADS, GLA_DK, GLA_DV), F32)
    _, s_start = lax.scan(step, state0, (gamma, delta))
    o_inter = jnp.einsum('bhncd,nbhdv->bhncv', q_dec, s_start)
    o = (o_intra + o_inter).transpose(0, 2, 3, 1, 4).reshape(bsz, s, GLA_HEADS, GLA_DV)
    o = o * lax.rsqrt(jnp.mean(o * o, axis=-1, keepdims=True) + EPS) * head_g.astype(F32)
    return o.reshape(bsz, s, GLA_WIDTH) * jax.nn.silu(og.astype(F32))


def swiglu(h, w1, w3, w2):
    return (jax.nn.silu(h @ w1) * (h @ w3)) @ w2


def moe_swiglu(h, w_router, w1, w3, w2):
    bsz, s, d = h.shape
    hf = h.reshape(bsz * s, d)
    logits = hf.astype(F32) @ w_router.astype(F32)
    top_v, top_i = lax.top_k(logits, TOP_K)
    gates = jax.nn.softmax(top_v, axis=-1)
    combine = jnp.sum(jax.nn.one_hot(top_i, N_EXPERTS, dtype=F32) * gates[..., None], axis=1)
    out = jnp.zeros((bsz * s, d), F32)
    for e in range(N_EXPERTS):
        out = out + combine[:, e:e + 1] * swiglu(hf, w1[e], w3[e], w2[e]).astype(F32)
    return out.reshape(bsz, s, d).astype(h.dtype)


def setup_inputs(seed: int = 0) -> dict:
    key = jax.random.key(seed)
    ks = iter(jax.random.split(key, 64))

    def nrm(shape, scale):
        return jax.random.normal(next(ks), shape, F32) * scale

    def gain(shape):
        return 1.0 + 0.02 * jax.random.normal(next(ks), shape, F32)

    L = DEPTH
    ND = (DEPTH + 1) // 2
    NM = DEPTH // 2
    G, P = S5_GROUPS, S5_STATE
    x = nrm((BATCH, SEQ, D_MODEL), 1.0)
    c = nrm((BATCH, D_MODEL), 1.0)
    ada_w = nrm((L, D_MODEL, N_ADA * D_MODEL), 0.5 * D_MODEL ** -0.5)
    ada_b = nrm((L, N_ADA * D_MODEL), 0.02)
    norm_mix_g = gain((L, D_MODEL))
    norm_ffn_g = gain((L, D_MODEL))
    w_in = nrm((L, D_MODEL, D_IN), D_MODEL ** -0.5)
    s5_a_re = -0.5 + nrm((L, G, P), 0.01)
    s5_a_im = math.pi * jnp.arange(P, dtype=F32) + nrm((L, G, P), 0.01)
    s5_log_dt = jax.random.uniform(next(ks), (L, G), F32, math.log(1e-3), math.log(1e-1))
    s5_b_re = nrm((L, G, P, S5_GROUP), (2.0 * S5_GROUP) ** -0.5)
    s5_b_im = nrm((L, G, P, S5_GROUP), (2.0 * S5_GROUP) ** -0.5)
    s5_c_re = nrm((L, G, S5_GROUP, P), (2.0 * P) ** -0.5)
    s5_c_im = nrm((L, G, S5_GROUP, P), (2.0 * P) ** -0.5)
    s5_d = nrm((L, S5_WIDTH), 1.0)
    s5_w_glu = nrm((L, S5_WIDTH, 2 * S5_WIDTH), S5_WIDTH ** -0.5)
    s5_b_glu = nrm((L, 2 * S5_WIDTH), 0.02)
    s5_norm_g = gain((L, S5_WIDTH))
    lru_conv_w = nrm((L, CONV_WIDTH, LRU_WIDTH), CONV_WIDTH ** -0.5)
    lru_conv_b = nrm((L, LRU_WIDTH), 0.02)
    lru_w_a = nrm((L, LRU_HEADS, LRU_HEAD_DIM, LRU_HEAD_DIM), LRU_HEAD_DIM ** -0.5)
    lru_b_a = nrm((L, LRU_WIDTH), 0.02)
    lru_w_x = nrm((L, LRU_HEADS, LRU_HEAD_DIM, LRU_HEAD_DIM), LRU_HEAD_DIM ** -0.5)
    lru_b_x = nrm((L, LRU_WIDTH), 0.02)
    a_pow_c = jax.random.uniform(next(ks), (L, LRU_WIDTH), F32, 0.9, 0.999)
    a0 = a_pow_c ** (1.0 / LRU_C)
    lru_lambda = jnp.log(a0) - jnp.log1p(-a0)
    lru_norm_g = gain((L, LRU_WIDTH))
    gla_w_g2 = nrm((L, GLA_GATE_RANK, GLA_HEADS * GLA_DK), GLA_GATE_RANK ** -0.5)
    gla_b_g2 = nrm((L, GLA_HEADS * GLA_DK), 0.02)
    gla_head_g = gain((L, GLA_DV))
    w_out = nrm((L, D_MIX, D_MODEL), D_MIX ** -0.5)
    ffn_w1 = nrm((ND, D_MODEL, D_FF), D_MODEL ** -0.5)
    ffn_w3 = nrm((ND, D_MODEL, D_FF), D_MODEL ** -0.5)
    ffn_w2 = nrm((ND, D_FF, D_MODEL), D_FF ** -0.5)
    moe_router = nrm((NM, D_MODEL, N_EXPERTS), D_MODEL ** -0.5)
    moe_w1 = nrm((NM, N_EXPERTS, D_MODEL, D_FF_EXPERT), D_MODEL ** -0.5)
    moe_w3 = nrm((NM, N_EXPERTS, D_MODEL, D_FF_EXPERT), D_MODEL ** -0.5)
    moe_w2 = nrm((NM, N_EXPERTS, D_FF_EXPERT, D_MODEL), D_FF_EXPERT ** -0.5)
    final_norm_g = gain((D_MODEL,))
    return {"x": x, "c": c, "ada_w": ada_w, "ada_b": ada_b, "norm_mix_g": norm_mix_g,
            "norm_ffn_g": norm_ffn_g, "w_in": w_in,
            "s5_a_re": s5_a_re, "s5_a_im": s5_a_im, "s5_log_dt": s5_log_dt,
            "s5_b_re": s5_b_re, "s5_b_im": s5_b_im, "s5_c_re": s5_c_re, "s5_c_im": s5_c_im,
            "s5_d": s5_d, "s5_w_glu": s5_w_glu, "s5_b_glu": s5_b_glu, "s5_norm_g": s5_norm_g,
            "lru_conv_w": lru_conv_w, "lru_conv_b": lru_conv_b, "lru_w_a": lru_w_a,
            "lru_b_a": lru_b_a, "lru_w_x": lru_w_x, "lru_b_x": lru_b_x,
            "lru_lambda": lru_lambda, "lru_norm_g": lru_norm_g,
            "gla_w_g2": gla_w_g2, "gla_b_g2": gla_b_g2, "gla_head_g": gla_head_g,
            "w_out": w_out, "ffn_w1": ffn_w1, "ffn_w3": ffn_w3, "ffn_w2": ffn_w2,
            "moe_router": moe_router, "moe_w1": moe_w1, "moe_w3": moe_w3, "moe_w2": moe_w2,
            "final_norm_g": final_norm_g}


def reference(x, c, ada_w, ada_b, norm_mix_g, norm_ffn_g, w_in,
              s5_a_re, s5_a_im, s5_log_dt, s5_b_re, s5_b_im, s5_c_re, s5_c_im,
              s5_d, s5_w_glu, s5_b_glu, s5_norm_g,
              lru_conv_w, lru_conv_b, lru_w_a, lru_b_a, lru_w_x, lru_b_x, lru_lambda, lru_norm_g,
              gla_w_g2, gla_b_g2, gla_head_g, w_out,
              ffn_w1, ffn_w3, ffn_w2, moe_router, moe_w1, moe_w3, moe_w2, final_norm_g):
    bsz = x.shape[0]
    cs = jax.nn.silu(c)
    for l in range(DEPTH):
        mod = (cs @ ada_w[l] + ada_b[l]).reshape(bsz, N_ADA, 1, D_MODEL)
        shift1, scale1, gate1, shift2, scale2, gate2 = (mod[:, i] for i in range(N_ADA))
        h = rmsnorm(x, norm_mix_g[l]) * (1.0 + scale1) + shift1
        proj = h @ w_in[l]
        u_s5, x_lru, g_lru, q, k, v, g_low, og = jnp.split(proj, IN_SPLITS, axis=-1)
        y_s5 = rmsnorm(s5_mixer(u_s5, s5_a_re[l], s5_a_im[l], s5_log_dt[l], s5_b_re[l], s5_b_im[l],
                                s5_c_re[l], s5_c_im[l], s5_d[l], s5_w_glu[l], s5_b_glu[l]), s5_norm_g[l])
        y_lru = rmsnorm(rglru_mixer(x_lru, g_lru, lru_conv_w[l], lru_conv_b[l], lru_w_a[l], lru_b_a[l],
                                    lru_w_x[l], lru_b_x[l], lru_lambda[l]), lru_norm_g[l])
        y_gla = gla_mixer(q, k, v, g_low, og, gla_w_g2[l], gla_b_g2[l], gla_head_g[l])
        mix = jnp.concatenate([y_s5, y_lru, y_gla], axis=-1).astype(x.dtype) @ w_out[l]
        x = x + gate1 * mix
        h = rmsnorm(x, norm_ffn_g[l]) * (1.0 + scale2) + shift2
        if l % 2 == 0:
            f = swiglu(h, ffn_w1[l // 2], ffn_w3[l // 2], ffn_w2[l // 2])
        else:
            f = moe_swiglu(h, moe_router[l // 2], moe_w1[l // 2], moe_w3[l // 2], moe_w2[l // 2])
        x = x + gate2 * f
    return rmsnorm(x, final_norm_g)
```

```python
import functools
import math

import jax
import jax.numpy as jnp
from jax import lax
from jax.experimental import pallas as pl
from jax.experimental.pallas import tpu as pltpu

F32 = jnp.float32
BF16 = jnp.bfloat16

D_MODEL = 2048
S5_WIDTH = 512
S5_GROUP = 16
S5_GROUPS = 32
S5_STATE = 64
S5_NSTATE = S5_GROUPS * S5_STATE
LRU_WIDTH = 768
LRU_HEADS = 8
LRU_HEAD_DIM = 96
CONV_WIDTH = 4
LRU_C = 8.0
GLA_WIDTH = 768
GLA_HEADS = 4
GLA_DV = 192
GLA_DK = 96
GLA_GATE_RANK = 16
GLA_TAU = 16.0
GLA_CHUNK = 64
N_EXPERTS = 8
N_ADA = 6
EPS = 1e-6

LANE = 128
SUBLANE = 8
GLA_DKP = 128
GLA_DVP = 256
GLA_QW = GLA_HEADS * GLA_DKP
GLA_VW = GLA_HEADS * GLA_DVP
VMEM_LIMIT = 56 * 1024 * 1024

COL_LRU_X = 0
COL_LRU_G = 768
COL_S5 = 1536
COL_Q = 2048
COL_K = 2560
COL_V = 3072
COL_OG = 4096
COL_GLOW = 5120
NP_IN = 5248
NP_OUT_ROWS = S5_WIDTH + LRU_WIDTH + GLA_VW

TS = 512
NSEG = SUBLANE
SEG = TS // NSEG

TM_PROJ = 256
TM_FFN = 512
TM_DOWN = 256
TN_FFN = 512
TM_ROUTE = 512
TM_COMB = 256


def _cparams(sem, vmem=VMEM_LIMIT):
    return pltpu.CompilerParams(dimension_semantics=sem, vmem_limit_bytes=vmem)


def _dot(a, b):
    return jnp.dot(a, b, preferred_element_type=F32)


def _resident(shape):
    nd = len(shape)
    return pl.BlockSpec(shape, lambda *_: (0,) * nd, pipeline_mode=pl.Buffered(1))


def _mod_kernel(c_ref, w_ref, b_ref, o_ref):
    c = c_ref[...]
    cs = (c * jax.nn.sigmoid(c)).astype(BF16)
    o_ref[...] = _dot(cs, w_ref[...].astype(BF16)) + b_ref[...]


def _adaln_mod(c, ada_w, ada_b):
    depth, d, n = ada_w.shape
    bsz = c.shape[0]
    tn = 1024
    return pl.pallas_call(
        _mod_kernel,
        grid=(depth, n // tn),
        in_specs=[
            pl.BlockSpec((bsz, d), lambda l, j: (0, 0)),
            pl.BlockSpec((None, d, tn), lambda l, j: (l, 0, j)),
            pl.BlockSpec((None, 1, tn), lambda l, j: (l, 0, j)),
        ],
        out_specs=pl.BlockSpec((None, bsz, tn), lambda l, j: (l, 0, j)),
        out_shape=jax.ShapeDtypeStruct((depth, bsz, n), F32),
        compiler_params=_cparams(("arbitrary", "arbitrary")),
        name="adaln_mod",
    )(c, ada_w, ada_b.reshape(depth, 1, n))


def _mod_spec(layer, bsz, which, tiles_per_batch):
    def imap(i, *_):
        return ((layer * bsz + i // tiles_per_batch) * N_ADA + which, 0, 0)
    return pl.BlockSpec((None, 1, D_MODEL), imap)


def _modulated_norm(x, g, scale, shift):
    ms = jnp.mean(x * x, axis=-1, keepdims=True)
    h = x * lax.rsqrt(ms + EPS) * g
    return h * (1.0 + scale) + shift


_IN_CHUNKS = tuple((s, min(512, NP_IN - s)) for s in range(0, NP_IN, 512))


def _inproj_kernel(x_ref, g_ref, sh_ref, sc_ref, w_ref, o_ref):
    h = _modulated_norm(x_ref[...], g_ref[...], sc_ref[...], sh_ref[...]).astype(BF16)
    for s, w in _IN_CHUNKS:
        o_ref[:, s:s + w] = _dot(h, w_ref[:, s:s + w])


def _in_projection(x2, g, mod3, w_packed, layer, bsz, seq):
    t = x2.shape[0]
    tpb = seq // TM_PROJ
    return pl.pallas_call(
        _inproj_kernel,
        grid=(t // TM_PROJ,),
        in_specs=[
            pl.BlockSpec((TM_PROJ, D_MODEL), lambda i: (i, 0)),
            pl.BlockSpec((1, D_MODEL), lambda i: (0, 0)),
            _mod_spec(layer, bsz, 0, tpb),
            _mod_spec(layer, bsz, 1, tpb),
            _resident((D_MODEL, NP_IN)),
        ],
        out_specs=pl.BlockSpec((TM_PROJ, NP_IN), lambda i: (i, 0)),
        out_shape=jax.ShapeDtypeStruct((t, NP_IN), F32),
        compiler_params=_cparams(("arbitrary",)),
        name="in_projection",
    )(x2, g.reshape(1, D_MODEL), mod3, mod3, w_packed)


S5_LW = 512
S5_NCHUNK = S5_NSTATE // S5_LW
S5_CW = S5_WIDTH // S5_NCHUNK
S5_LT = S5_LW // LANE


def _s5_kernel(u_ref, bre_ref, bim_ref, cre_ref, cim_ref, lam_ref, lseg_ref, d_ref, wg_ref, bg_ref,
               ng_ref, o_ref, hr_ref, hi_ref, cr_ref, ci_ref):
    @pl.when(pl.program_id(1) == 0)
    def _():
        cr_ref[...] = jnp.zeros_like(cr_ref)
        ci_ref[...] = jnp.zeros_like(ci_ref)

    u = u_ref[...]
    ub = u.astype(BF16)
    ys = []
    for j in range(S5_NCHUNK):
        uj = ub[:, j * S5_CW:(j + 1) * S5_CW]
        bu_r = _dot(uj, bre_ref[j])
        bu_i = _dot(uj, bim_ref[j])
        lanes = [slice(j * S5_LW + q * LANE, j * S5_LW + (q + 1) * LANE) for q in range(S5_LT)]
        for q in range(S5_LT):
            hr_ref[q] = bu_r[:, q * LANE:(q + 1) * LANE]
            hi_ref[q] = bu_i[:, q * LANE:(q + 1) * LANE]
        lr = [jnp.broadcast_to(lam_ref[0:1, ls], (NSEG, LANE)) for ls in lanes]
        li = [jnp.broadcast_to(lam_ref[1:2, ls], (NSEG, LANE)) for ls in lanes]

        def local_scan(k, carry):
            rows = pl.ds(k, NSEG, stride=SEG)
            out = []
            for q in range(S5_LT):
                hr, hi = carry[q]
                nr = lr[q] * hr - li[q] * hi + hr_ref[q, rows, :]
                ni = lr[q] * hi + li[q] * hr + hi_ref[q, rows, :]
                hr_ref[q, rows, :] = nr
                hi_ref[q, rows, :] = ni
                out.append((nr, ni))
            return tuple(out)

        zero = jnp.zeros((NSEG, LANE), F32)
        ends = lax.fori_loop(0, SEG, local_scan, tuple((zero, zero) for _ in range(S5_LT)))

        starts = []
        for q in range(S5_LT):
            er, ei = ends[q]
            pr = lseg_ref[0:1, lanes[q]]
            pi = lseg_ref[1:2, lanes[q]]
            s_r = cr_ref[:, lanes[q]]
            s_i = ci_ref[:, lanes[q]]
            st_r, st_i = [], []
            for m in range(NSEG):
                st_r.append(s_r)
                st_i.append(s_i)
                s_r, s_i = (er[m:m + 1] + pr * s_r - pi * s_i,
                            ei[m:m + 1] + pr * s_i + pi * s_r)
            cr_ref[:, lanes[q]] = s_r
            ci_ref[:, lanes[q]] = s_i
            starts.append((jnp.concatenate(st_r, axis=0), jnp.concatenate(st_i, axis=0)))

        def correct(k, carry):
            rows = pl.ds(k, NSEG, stride=SEG)
            out = []
            for q in range(S5_LT):
                c_r, c_i = carry[q]
                n_r = lr[q] * c_r - li[q] * c_i
                n_i = lr[q] * c_i + li[q] * c_r
                hr_ref[q, rows, :] = hr_ref[q, rows, :] + n_r
                hi_ref[q, rows, :] = hi_ref[q, rows, :] + n_i
                out.append((n_r, n_i))
            return tuple(out)

        lax.fori_loop(0, SEG, correct, tuple(starts))
        h_r = jnp.concatenate([hr_ref[q] for q in range(S5_LT)], axis=1).astype(BF16)
        h_i = jnp.concatenate([hi_ref[q] for q in range(S5_LT)], axis=1).astype(BF16)
        ys.append(_dot(h_r, cre_ref[j]) - _dot(h_i, cim_ref[j]))

    y = jnp.concatenate(ys, axis=1) + d_ref[...] * u
    z = jax.nn.gelu(y).astype(BF16)
    zz = _dot(z, wg_ref[...]) + bg_ref[...]
    o = zz[:, :S5_WIDTH] * jax.nn.sigmoid(zz[:, S5_WIDTH:])
    o = o * lax.rsqrt(jnp.mean(o * o, axis=-1, keepdims=True) + EPS) * ng_ref[...]
    o_ref[...] = o.astype(BF16)


def _s5_mixer(proj, p, bsz, seq):
    t = bsz * seq
    nb = seq // TS
    ublk = COL_S5 // S5_WIDTH
    return pl.pallas_call(
        _s5_kernel,
        grid=(bsz, nb),
        in_specs=[
            pl.BlockSpec((TS, S5_WIDTH), lambda b, s: (b * nb + s, ublk)),
            _resident((S5_NCHUNK, S5_CW, S5_LW)),
            _resident((S5_NCHUNK, S5_CW, S5_LW)),
            _resident((S5_NCHUNK, S5_LW, S5_CW)),
            _resident((S5_NCHUNK, S5_LW, S5_CW)),
            _resident((2, S5_NSTATE)),
            _resident((2, S5_NSTATE)),
            _resident((1, S5_WIDTH)),
            _resident((S5_WIDTH, 2 * S5_WIDTH)),
            _resident((1, 2 * S5_WIDTH)),
            _resident((1, S5_WIDTH)),
        ],
        out_specs=pl.BlockSpec((TS, S5_WIDTH), lambda b, s: (b * nb + s, 0)),
        out_shape=jax.ShapeDtypeStruct((t, S5_WIDTH), BF16),
        scratch_shapes=[
            pltpu.VMEM((S5_LT, TS, LANE), F32),
            pltpu.VMEM((S5_LT, TS, LANE), F32),
            pltpu.VMEM((1, S5_NSTATE), F32),
            pltpu.VMEM((1, S5_NSTATE), F32),
        ],
        compiler_params=_cparams(("arbitrary", "arbitrary")),
        name="s5_mixer",
    )(proj, p["bbd_re"], p["bbd_im"], p["cbd_re"], p["cbd_im"], p["lam"], p["lam_seg"],
      p["d"], p["w_glu"], p["b_glu"], p["norm_g"])


def _s5_params(a_re, a_im, log_dt, b_re, b_im, c_re, c_im, d_skip, w_glu, b_glu, norm_g):
    gpc = S5_GROUPS // S5_NCHUNK
    dt = jnp.exp(log_dt)[:, None]
    mag = jnp.exp(a_re * dt)
    abar_re = mag * jnp.cos(a_im * dt)
    abar_im = mag * jnp.sin(a_im * dt)
    nr = abar_re - 1.0
    ni = abar_im
    den = a_re * a_re + a_im * a_im
    f_re = (nr * a_re + ni * a_im) / den
    f_im = (ni * a_re - nr * a_im) / den
    bbar_re = f_re[..., None] * b_re - f_im[..., None] * b_im
    bbar_im = f_re[..., None] * b_im + f_im[..., None] * b_re
    eye = jnp.eye(gpc, dtype=F32)
    bbd = lambda m: jnp.einsum("jgpc,gh->jgchp", m.reshape(S5_NCHUNK, gpc, S5_STATE, S5_GROUP),
                               eye).reshape(S5_NCHUNK, S5_CW, S5_LW).astype(BF16)
    cbd = lambda m: jnp.einsum("jgcp,gh->jgphc", m.reshape(S5_NCHUNK, gpc, S5_GROUP, S5_STATE),
                               eye).reshape(S5_NCHUNK, S5_LW, S5_CW).astype(BF16)
    seg_mag = jnp.exp(SEG * a_re * dt)
    lam = jnp.stack([abar_re.reshape(-1), abar_im.reshape(-1)])
    lam_seg = jnp.stack([(seg_mag * jnp.cos(SEG * a_im * dt)).reshape(-1),
                         (seg_mag * jnp.sin(SEG * a_im * dt)).reshape(-1)])
    return {
        "bbd_re": bbd(bbar_re), "bbd_im": bbd(bbar_im),
        "cbd_re": cbd(c_re), "cbd_im": cbd(c_im),
        "lam": lam, "lam_seg": lam_seg,
        "d": d_skip.reshape(1, S5_WIDTH), "w_glu": w_glu.astype(BF16),
        "b_glu": b_glu.reshape(1, -1), "norm_g": norm_g.reshape(1, S5_WIDTH),
    }


LRU_LT = LRU_WIDTH // LANE


def _lru_kernel(x_ref, gb_ref, cw_ref, cb_ref, wa_ref, ba_ref, wx_ref, bx_ref, lam_ref, ng_ref,
                o_ref, xe_ref, a_ref, b_ref, hc_ref):
    first = pl.program_id(1) == 0

    @pl.when(first)
    def _():
        xe_ref[0:SUBLANE, :] = jnp.zeros((SUBLANE, LRU_WIDTH), F32)
        hc_ref[...] = jnp.zeros_like(hc_ref)

    @pl.when(jnp.logical_not(first))
    def _():
        xe_ref[0:SUBLANE, :] = xe_ref[TS:TS + SUBLANE, :]

    xe_ref[SUBLANE:TS + SUBLANE, :] = x_ref[...]
    xc = cb_ref[...]
    for w in range(CONV_WIDTH):
        off = SUBLANE - (CONV_WIDTH - 1) + w
        xc = xc + cw_ref[w:w + 1, :] * xe_ref[off:off + TS, :]
    xb = xc.astype(BF16)
    r = jax.nn.sigmoid(_dot(xb, wa_ref[...]) + ba_ref[...])
    gi = jax.nn.sigmoid(_dot(xb, wx_ref[...]) + bx_ref[...])
    log_a = LRU_C * r * jax.nn.log_sigmoid(lam_ref[...])
    a = jnp.exp(log_a)
    one_minus_a2 = -jnp.tanh(log_a) * (1.0 + a * a)
    bt = jnp.sqrt(jnp.maximum(one_minus_a2, 0.0)) * gi * xc
    for q in range(LRU_LT):
        a_ref[q] = a[:, q * LANE:(q + 1) * LANE]
        b_ref[q] = bt[:, q * LANE:(q + 1) * LANE]

    def local_scan(k, carry):
        rows = pl.ds(k, NSEG, stride=SEG)
        out = []
        for q in range(LRU_LT):
            h, ap = carry[q]
            aq = a_ref[q, rows, :]
            h = aq * h + b_ref[q, rows, :]
            ap = ap * aq
            b_ref[q, rows, :] = h
            a_ref[q, rows, :] = ap
            out.append((h, ap))
        return tuple(out)

    init = (jnp.zeros((NSEG, LANE), F32), jnp.ones((NSEG, LANE), F32))
    ends = lax.fori_loop(0, SEG, local_scan, tuple(init for _ in range(LRU_LT)))
    e = jnp.concatenate([ends[q][0] for q in range(LRU_LT)], axis=1)
    ap = jnp.concatenate([ends[q][1] for q in range(LRU_LT)], axis=1)

    s = hc_ref[...]
    g = gb_ref[...]
    for j in range(NSEG):
        rs = slice(j * SEG, (j + 1) * SEG)
        hl = jnp.concatenate([b_ref[q, rs, :] for q in range(LRU_LT)], axis=1)
        al = jnp.concatenate([a_ref[q, rs, :] for q in range(LRU_LT)], axis=1)
        h = hl + al * s
        o = h * jax.nn.gelu(g[rs, :])
        o = o * lax.rsqrt(jnp.mean(o * o, axis=-1, keepdims=True) + EPS) * ng_ref[...]
        o_ref[rs, :] = o.astype(BF16)
        s = e[j:j + 1] + ap[j:j + 1] * s
    hc_ref[...] = s


def _lru_mixer(proj, p, bsz, seq):
    t = bsz * seq
    nb = seq // TS
    vec = _resident((1, LRU_WIDTH))
    return pl.pallas_call(
        _lru_kernel,
        grid=(bsz, nb),
        in_specs=[
            pl.BlockSpec((TS, LRU_WIDTH), lambda b, s: (b * nb + s, COL_LRU_X // LRU_WIDTH)),
            pl.BlockSpec((TS, LRU_WIDTH), lambda b, s: (b * nb + s, COL_LRU_G // LRU_WIDTH)),
            _resident((CONV_WIDTH, LRU_WIDTH)), vec,
            _resident((LRU_WIDTH, LRU_WIDTH)), vec,
            _resident((LRU_WIDTH, LRU_WIDTH)), vec,
            vec, vec,
        ],
        out_specs=pl.BlockSpec((TS, LRU_WIDTH), lambda b, s: (b * nb + s, 0)),
        out_shape=jax.ShapeDtypeStruct((t, LRU_WIDTH), BF16),
        scratch_shapes=[
            pltpu.VMEM((TS + SUBLANE, LRU_WIDTH), F32),
            pltpu.VMEM((LRU_LT, TS, LANE), F32),
            pltpu.VMEM((LRU_LT, TS, LANE), F32),
            pltpu.VMEM((1, LRU_WIDTH), F32),
        ],
        compiler_params=_cparams(("arbitrary", "arbitrary")),
        name="rglru_mixer",
    )(proj, proj, p["conv_w"], p["conv_b"], p["wa"], p["ba"], p["wx"], p["bx"], p["lam"], p["norm_g"])


def _lru_params(conv_w, conv_b, w_a, b_a, w_x, b_x, lam, norm_g):
    eye = jnp.eye(LRU_HEADS, dtype=F32)
    bd = lambda m: jnp.einsum("hij,hk->hikj", m, eye).reshape(LRU_WIDTH, LRU_WIDTH).astype(BF16)
    row = lambda v: v.reshape(1, LRU_WIDTH)
    return {"conv_w": conv_w, "conv_b": row(conv_b), "wa": bd(w_a), "ba": row(b_a),
            "wx": bd(w_x), "bx": row(b_x), "lam": row(lam), "norm_g": row(norm_g)}


def _split3(x):
    hi = x.astype(BF16)
    r1 = x - hi.astype(F32)
    mid = r1.astype(BF16)
    lo = (r1 - mid.astype(F32)).astype(BF16)
    return hi, mid, lo


def _gla_kernel(q_ref, k_ref, v_ref, og_ref, gl_ref, wg_ref, bg_ref, hg_ref, o_ref, st_ref):
    @pl.when(pl.program_id(1) == 0)
    def _():
        st_ref[...] = jnp.zeros_like(st_ref)

    c = GLA_CHUNK
    hc = GLA_HEADS * c
    ri = lax.broadcasted_iota(jnp.int32, (c, c), 0)
    ci = lax.broadcasted_iota(jnp.int32, (c, c), 1)
    tri = (ri >= ci).astype(BF16)
    rr = lax.broadcasted_iota(jnp.int32, (hc, hc), 0)
    cc = lax.broadcasted_iota(jnp.int32, (hc, hc), 1)
    causal = jnp.logical_and(rr // c == cc // c, rr % c >= cc % c)
    heads = lambda m, w: jnp.concatenate([m[:, h * w:(h + 1) * w] for h in range(GLA_HEADS)], axis=0)

    for n in range(TS // c):
        rs = slice(n * c, (n + 1) * c)
        logit = _dot(gl_ref[rs, :].astype(BF16), wg_ref[...]) + bg_ref[...]
        la = jax.nn.log_sigmoid(logit) / GLA_TAU
        hi, mid, lo = _split3(la)
        bcum = _dot(tri, hi) + _dot(tri, mid) + _dot(tri, lo)
        b_last = bcum[c - 1:c, :]
        kk = k_ref[rs, :]
        q_dec = q_ref[rs, :] * (GLA_DK ** -0.5) * jnp.exp(bcum)
        k_in = kk * jnp.exp(-bcum)
        k_out = (kk * jnp.exp(b_last - bcum)).astype(BF16)
        gamma = jnp.exp(b_last)
        vb = v_ref[rs, :].astype(BF16)
        qd = q_dec.astype(BF16)

        att = lax.dot_general(heads(qd, GLA_DKP), heads(k_in.astype(BF16), GLA_DKP),
                              (((1,), (1,)), ((), ())), preferred_element_type=F32)
        att = jnp.where(causal, att, 0.0).astype(BF16)
        o_s = _dot(att, heads(vb, GLA_DVP))
        outs = []
        for h in range(GLA_HEADS):
            ks = slice(h * GLA_DKP, (h + 1) * GLA_DKP)
            vs = slice(h * GLA_DVP, (h + 1) * GLA_DVP)
            st = st_ref[h]
            o_h = o_s[h * c:(h + 1) * c, :] + lax.dot_general(
                qd[:, ks], st.astype(BF16), (((1,), (1,)), ((), ())), preferred_element_type=F32)
            delta = lax.dot_general(vb[:, vs], k_out[:, ks], (((0,), (0,)), ((), ())),
                                    preferred_element_type=F32)
            st_ref[h] = gamma[:, ks] * st + delta
            ms = jnp.sum(o_h * o_h, axis=-1, keepdims=True) * (1.0 / GLA_DV)
            outs.append(o_h * lax.rsqrt(ms + EPS) * hg_ref[...])
        o = jnp.concatenate(outs, axis=1)
        og = og_ref[rs, :]
        o_ref[rs, :] = (o * (og * jax.nn.sigmoid(og))).astype(BF16)


def _gla_mixer(proj, p, bsz, seq):
    t = bsz * seq
    nb = seq // TS
    row = lambda b, s: b * nb + s
    return pl.pallas_call(
        _gla_kernel,
        grid=(bsz, nb),
        in_specs=[
            pl.BlockSpec((TS, GLA_QW), lambda b, s: (row(b, s), COL_Q // GLA_QW)),
            pl.BlockSpec((TS, GLA_QW), lambda b, s: (row(b, s), COL_K // GLA_QW)),
            pl.BlockSpec((TS, GLA_VW), lambda b, s: (row(b, s), COL_V // GLA_VW)),
            pl.BlockSpec((TS, GLA_VW), lambda b, s: (row(b, s), COL_OG // GLA_VW)),
            pl.BlockSpec((TS, LANE), lambda b, s: (row(b, s), COL_GLOW // LANE)),
            _resident((LANE, GLA_QW)),
            _resident((1, GLA_QW)),
            _resident((1, GLA_DVP)),
        ],
        out_specs=pl.BlockSpec((TS, GLA_VW), lambda b, s: (row(b, s), 0)),
        out_shape=jax.ShapeDtypeStruct((t, GLA_VW), BF16),
        scratch_shapes=[pltpu.VMEM((GLA_HEADS, GLA_DVP, GLA_DKP), F32)],
        compiler_params=_cparams(("arbitrary", "arbitrary")),
        name="gla_mixer",
    )(proj, proj, proj, proj, proj, p["w_g2"], p["b_g2"], p["head_g"])


def _pad_heads(m, d, dp):
    lead = m.shape[:-1]
    m = m.reshape(lead + (GLA_HEADS, d))
    m = jnp.pad(m, [(0, 0)] * len(lead) + [(0, 0), (0, dp - d)])
    return m.reshape(lead + (GLA_HEADS * dp,))


def _gla_params(w_g2, b_g2, head_g):
    w = jnp.pad(_pad_heads(w_g2, GLA_DK, GLA_DKP), ((0, LANE - GLA_GATE_RANK), (0, 0)))
    return {"w_g2": w.astype(BF16),
            "b_g2": _pad_heads(b_g2, GLA_DK, GLA_DKP).reshape(1, GLA_QW),
            "head_g": jnp.pad(head_g, (0, GLA_DVP - GLA_DV)).reshape(1, GLA_DVP)}


def _pack_w_in(w):
    s5 = w[:, 0:512]
    lx = w[:, 512:1280]
    lg = w[:, 1280:2048]
    q = w[:, 2048:2432]
    k = w[:, 2432:2816]
    v = w[:, 2816:3584]
    gl = w[:, 3584:3600]
    og = w[:, 3600:4368]
    cols = [lx, lg, s5, _pad_heads(q, GLA_DK, GLA_DKP), _pad_heads(k, GLA_DK, GLA_DKP),
            _pad_heads(v, GLA_DV, GLA_DVP), _pad_heads(og, GLA_DV, GLA_DVP),
            jnp.pad(gl, ((0, 0), (0, LANE - GLA_GATE_RANK)))]
    return jnp.concatenate(cols, axis=1).astype(BF16)


def _pack_w_out(w):
    gla = w[S5_WIDTH + LRU_WIDTH:].reshape(GLA_HEADS, GLA_DV, D_MODEL)
    gla = jnp.pad(gla, ((0, 0), (0, GLA_DVP - GLA_DV), (0, 0))).reshape(GLA_VW, D_MODEL)
    return jnp.concatenate([w[:S5_WIDTH + LRU_WIDTH], gla], axis=0).astype(BF16)


def _outproj_kernel(ys_ref, yl_ref, yg_ref, x_ref, w_ref, gate_ref, g_ref, sh_ref, sc_ref, *rest,
                    with_router):
    r0, r1 = S5_WIDTH, S5_WIDTH + LRU_WIDTH
    mix = (_dot(ys_ref[...], w_ref[0:r0, :]) + _dot(yl_ref[...], w_ref[r0:r1, :])
           + _dot(yg_ref[...], w_ref[r1:NP_OUT_ROWS, :]))
    x1 = x_ref[...] + gate_ref[...] * mix
    h = _modulated_norm(x1, g_ref[...], sc_ref[...], sh_ref[...])
    if with_router:
        wr_hi_ref, wr_lo_ref, x1_ref, h_ref, lg_ref = rest
        h_ref[...] = h
        hh = h.astype(BF16)
        hm = (h - hh.astype(F32)).astype(BF16)
        lg_ref[...] = (_dot(hh, wr_hi_ref[...]) + _dot(hm, wr_hi_ref[...])
                       + _dot(hh, wr_lo_ref[...]))
    else:
        x1_ref, h_ref = rest
        h_ref[...] = h.astype(BF16)
    x1_ref[...] = x1


def _out_projection(ys, yl, yg, x2, w_packed, g, mod3, layer, bsz, seq, router=None):
    t = x2.shape[0]
    tpb = seq // TM_PROJ
    rowblk = lambda w: pl.BlockSpec((TM_PROJ, w), lambda i: (i, 0))
    in_specs = [rowblk(S5_WIDTH), rowblk(LRU_WIDTH), rowblk(GLA_VW), rowblk(D_MODEL),
                _resident((NP_OUT_ROWS, D_MODEL)),
                _mod_spec(layer, bsz, 2, tpb),
                pl.BlockSpec((1, D_MODEL), lambda i: (0, 0)),
                _mod_spec(layer, bsz, 3, tpb), _mod_spec(layer, bsz, 4, tpb)]
    args = [ys, yl, yg, x2, w_packed, mod3, g.reshape(1, D_MODEL), mod3, mod3]
    if router is None:
        out_specs = [rowblk(D_MODEL), rowblk(D_MODEL)]
        out_shape = [jax.ShapeDtypeStruct((t, D_MODEL), F32), jax.ShapeDtypeStruct((t, D_MODEL), BF16)]
    else:
        wr = jnp.pad(router, ((0, 0), (0, LANE - N_EXPERTS)))
        wr_hi = wr.astype(BF16)
        wr_lo = (wr - wr_hi.astype(F32)).astype(BF16)
        in_specs += [_resident((D_MODEL, LANE)), _resident((D_MODEL, LANE))]
        args += [wr_hi, wr_lo]
        out_specs = [rowblk(D_MODEL), rowblk(D_MODEL), rowblk(LANE)]
        out_shape = [jax.ShapeDtypeStruct((t, D_MODEL), F32), jax.ShapeDtypeStruct((t, D_MODEL), F32),
                     jax.ShapeDtypeStruct((t, LANE), F32)]
    return pl.pallas_call(
        functools.partial(_outproj_kernel, with_router=router is not None),
        grid=(t // TM_PROJ,),
        in_specs=in_specs, out_specs=out_specs, out_shape=out_shape,
        compiler_params=_cparams(("arbitrary",)),
        name="out_projection",
    )(*args)


def _gateup_kernel(te_ref, tf_ref, tv_ref, ti_ref, x_ref, w1_ref, w3_ref, o_ref, w1b_ref, w3b_ref):
    i = pl.program_id(1)

    @pl.when(tf_ref[i] == 1)
    def _():
        w1b_ref[...] = w1_ref[...].astype(BF16)
        w3b_ref[...] = w3_ref[...].astype(BF16)

    @pl.when(tv_ref[i] == 1)
    def _():
        x = x_ref[...]
        h1 = _dot(x, w1b_ref[...])
        h3 = _dot(x, w3b_ref[...])
        o_ref[...] = (h1 * jax.nn.sigmoid(h1) * h3).astype(BF16)

    @pl.when(tv_ref[i] == 0)
    def _():
        o_ref[...] = jnp.zeros_like(o_ref)


def _ffn_gateup(tables, xs, w1, w3):
    r, d = xs.shape
    n_e, _, dff = w1.shape
    nt = r // TM_FFN
    grid_spec = pltpu.PrefetchScalarGridSpec(
        num_scalar_prefetch=4,
        grid=(dff // TN_FFN, nt),
        in_specs=[
            pl.BlockSpec((TM_FFN, d), lambda j, i, te, tf, tv, ti: (ti[i], 0)),
            pl.BlockSpec((None, d, TN_FFN), lambda j, i, te, tf, tv, ti: (te[i], 0, j)),
            pl.BlockSpec((None, d, TN_FFN), lambda j, i, te, tf, tv, ti: (te[i], 0, j)),
        ],
        out_specs=pl.BlockSpec((TM_FFN, TN_FFN), lambda j, i, te, tf, tv, ti: (i, j)),
        scratch_shapes=[pltpu.VMEM((d, TN_FFN), BF16), pltpu.VMEM((d, TN_FFN), BF16)],
    )
    return pl.pallas_call(
        _gateup_kernel,
        grid_spec=grid_spec,
        out_shape=jax.ShapeDtypeStruct((r, dff), BF16),
        compiler_params=_cparams(("arbitrary", "arbitrary")),
        name="ffn_gate_up",
    )(*tables, xs, w1, w3)


def _down_kernel(te_ref, tf_ref, tv_ref, ti_ref, a_ref, w2_ref, *rest, residual):
    i = pl.program_id(1)
    sub = TM_FFN // TM_DOWN
    if residual:
        x1_ref, gate_ref, o_ref, wb_ref = rest
    else:
        o_ref, wb_ref = rest

    @pl.when(jnp.logical_and(tf_ref[i // sub] == 1, i % sub == 0))
    def _():
        wb_ref[...] = w2_ref[...].astype(BF16)

    @pl.when(tv_ref[i // sub] == 1)
    def _():
        y = _dot(a_ref[...], wb_ref[...])
        if residual:
            y = x1_ref[...] + gate_ref[...] * y
        o_ref[...] = y

    @pl.when(tv_ref[i // sub] == 0)
    def _():
        o_ref[...] = jnp.zeros_like(o_ref)


def _ffn_down(tables, a, w2, residual=None):
    r, dff = a.shape
    d = w2.shape[2]
    sub = TM_FFN // TM_DOWN
    nt = r // TM_DOWN

    def rowidx(i, ti, tv):
        return ti[i // sub] * sub + jnp.where(tv[i // sub] == 1, i % sub, sub - 1)

    in_specs = [
        pl.BlockSpec((TM_DOWN, dff), lambda j, i, te, tf, tv, ti: (rowidx(i, ti, tv), 0)),
        pl.BlockSpec((None, dff, TN_FFN), lambda j, i, te, tf, tv, ti: (te[i // sub], 0, j)),
    ]
    args = [a, w2]
    if residual is not None:
        x1, mod3, layer, bsz, seq = residual
        tpb = seq // TM_DOWN
        in_specs += [
            pl.BlockSpec((TM_DOWN, TN_FFN), lambda j, i, te, tf, tv, ti: (i, j)),
            pl.BlockSpec((None, 1, TN_FFN),
                         lambda j, i, te, tf, tv, ti: ((layer * bsz + i // tpb) * N_ADA + 5, 0, j)),
        ]
        args += [x1, mod3]
    grid_spec = pltpu.PrefetchScalarGridSpec(
        num_scalar_prefetch=4,
        grid=(d // TN_FFN, nt),
        in_specs=in_specs,
        out_specs=pl.BlockSpec((TM_DOWN, TN_FFN), lambda j, i, te, tf, tv, ti: (i, j)),
        scratch_shapes=[pltpu.VMEM((dff, TN_FFN), BF16)],
    )
    return pl.pallas_call(
        functools.partial(_down_kernel, residual=residual is not None),
        grid_spec=grid_spec,
        out_shape=jax.ShapeDtypeStruct((r, d), F32),
        compiler_params=_cparams(("arbitrary", "arbitrary")),
        name="ffn_down",
    )(*tables, *args)


def _dense_tables(t):
    nt = t // TM_FFN
    ids = jnp.arange(nt, dtype=jnp.int32)
    return (jnp.zeros((nt,), jnp.int32), (ids == 0).astype(jnp.int32),
            jnp.ones((nt,), jnp.int32), ids)


def _route_kernel(lg_ref, o_ref, cnt_ref, carry_ref):
    @pl.when(pl.program_id(0) == 0)
    def _():
        carry_ref[...] = jnp.zeros_like(carry_ref)

    tm = TM_ROUTE
    lane = lax.broadcasted_iota(jnp.int32, (tm, LANE), 1)
    neg = jnp.float32(-jnp.inf)
    l = jnp.where(lane < N_EXPERTS, lg_ref[...], neg)
    m1 = jnp.max(l, axis=1, keepdims=True)
    i1 = jnp.min(jnp.where(l == m1, lane, LANE), axis=1, keepdims=True)
    oh1 = lane == i1
    l2 = jnp.where(oh1, neg, l)
    m2 = jnp.max(l2, axis=1, keepdims=True)
    i2 = jnp.min(jnp.where(l2 == m2, lane, LANE), axis=1, keepdims=True)
    oh2 = lane == i2
    e = jnp.exp(m2 - m1)
    den = 1.0 + e
    g1 = 1.0 / den
    g2 = e / den
    oh = jnp.where(jnp.logical_or(oh1, oh2), 1.0, 0.0)
    ri = lax.broadcasted_iota(jnp.int32, (tm, tm), 0)
    ci = lax.broadcasted_iota(jnp.int32, (tm, tm), 1)
    before = (ri > ci).astype(BF16)
    cum = _dot(before, oh.astype(BF16)) + carry_ref[...]
    r1 = jnp.sum(jnp.where(oh1, cum, 0.0), axis=1, keepdims=True)
    r2 = jnp.sum(jnp.where(oh2, cum, 0.0), axis=1, keepdims=True)
    carry_ref[...] = carry_ref[...] + jnp.sum(oh, axis=0, keepdims=True)
    cnt_ref[...] = carry_ref[...]
    out = jnp.where(lane == 0, i1.astype(F32), 0.0)
    out = jnp.where(lane == 1, i2.astype(F32), out)
    out = jnp.where(lane == 2, g1, out)
    out = jnp.where(lane == 3, g2, out)
    out = jnp.where(lane == 4, r1, out)
    out = jnp.where(lane == 5, r2, out)
    o_ref[...] = out


def _route(logits):
    t = logits.shape[0]
    return pl.pallas_call(
        _route_kernel,
        grid=(t // TM_ROUTE,),
        in_specs=[pl.BlockSpec((TM_ROUTE, LANE), lambda i: (i, 0))],
        out_specs=[pl.BlockSpec((TM_ROUTE, LANE), lambda i: (i, 0)),
                   pl.BlockSpec((1, LANE), lambda i: (0, 0))],
        out_shape=[jax.ShapeDtypeStruct((t, LANE), F32), jax.ShapeDtypeStruct((1, LANE), F32)],
        scratch_shapes=[pltpu.VMEM((1, LANE), F32)],
        compiler_params=_cparams(("arbitrary",)),
        name="route_top2",
    )(logits)


def _expert_tables(route, counts, t):
    nt = (2 * t) // TM_FFN + N_EXPERTS
    i1 = route[:, 0].astype(jnp.int32)
    i2 = route[:, 1].astype(jnp.int32)
    r1 = route[:, 4].astype(jnp.int32)
    r2 = route[:, 5].astype(jnp.int32)
    cnt = counts[0, :N_EXPERTS].astype(jnp.int32)
    tiles = (cnt + TM_FFN - 1) // TM_FFN
    tile_end = jnp.cumsum(tiles)
    row_off = (tile_end - tiles) * TM_FFN
    n_valid = tile_end[-1]
    ids = jnp.arange(nt, dtype=jnp.int32)
    ti = jnp.minimum(ids, n_valid - 1)
    te = jnp.minimum(jnp.searchsorted(tile_end, ti, side="right"), N_EXPERTS - 1).astype(jnp.int32)
    tv = (ids < n_valid).astype(jnp.int32)
    prev = jnp.concatenate([jnp.full((1,), -1, jnp.int32), te[:-1]])
    tf = jnp.logical_and(tv == 1, te != prev).astype(jnp.int32)
    pos1 = row_off[i1] + r1
    pos2 = row_off[i2] + r2
    tok = jnp.arange(t, dtype=jnp.int32)
    src = jnp.zeros((nt * TM_FFN,), jnp.int32).at[pos1].set(tok).at[pos2].set(tok)
    return (te, tf, tv, ti), pos1, pos2, src


def _row_copy(src_hbm, row, dst, slot, sem):
    return pltpu.make_async_copy(src_hbm.at[pl.ds(row, 1), :], dst.at[pl.ds(slot, 1), :], sem)


def _gather_kernel(src_ref, tv_ref, ti_ref, h_hbm, o_ref, buf_ref, sem):
    i = pl.program_id(0)

    @pl.when(tv_ref[i] == 1)
    def _():
        base = i * TM_FFN

        def issue(r, c):
            _row_copy(h_hbm, src_ref[base + r], buf_ref, r, sem).start()
            return c

        lax.fori_loop(0, TM_FFN, issue, 0)

        def drain(r, c):
            _row_copy(h_hbm, 0, buf_ref, r, sem).wait()
            return c

        lax.fori_loop(0, TM_FFN, drain, 0)
        o_ref[...] = buf_ref[...].astype(BF16)

    @pl.when(tv_ref[i] == 0)
    def _():
        o_ref[...] = jnp.zeros_like(o_ref)


def _gather_rows(src, tv, ti, h):
    nt = tv.shape[0]
    d = h.shape[1]
    grid_spec = pltpu.PrefetchScalarGridSpec(
        num_scalar_prefetch=3,
        grid=(nt,),
        in_specs=[pl.BlockSpec(memory_space=pl.ANY)],
        out_specs=pl.BlockSpec((TM_FFN, d), lambda i, src, tv, ti: (i, 0)),
        scratch_shapes=[pltpu.VMEM((TM_FFN, d), F32), pltpu.SemaphoreType.DMA(())],
    )
    return pl.pallas_call(
        _gather_kernel,
        grid_spec=grid_spec,
        out_shape=jax.ShapeDtypeStruct((nt * TM_FFN, d), BF16),
        compiler_params=_cparams(("arbitrary",)),
        name="expert_gather",
    )(src, tv, ti, h)


def _combine_kernel(p1_ref, p2_ref, y_hbm, x1_ref, gate_ref, rt_ref, fg_ref, o_ref, ya_ref, yb_ref, sem):
    base = pl.program_id(0) * TM_COMB

    def issue(r, c):
        _row_copy(y_hbm, p1_ref[base + r], ya_ref, r, sem).start()
        _row_copy(y_hbm, p2_ref[base + r], yb_ref, r, sem).start()
        return c

    lax.fori_loop(0, TM_COMB, issue, 0)

    def drain(r, c):
        _row_copy(y_hbm, 0, ya_ref, r, sem).wait()
        _row_copy(y_hbm, 0, yb_ref, r, sem).wait()
        return c

    lax.fori_loop(0, TM_COMB, drain, 0)
    rt = rt_ref[...]
    f = rt[:, 2:3] * ya_ref[...] + rt[:, 3:4] * yb_ref[...]
    x2 = x1_ref[...] + gate_ref[...] * f
    o_ref[...] = x2 * lax.rsqrt(jnp.mean(x2 * x2, axis=-1, keepdims=True) + EPS) * fg_ref[...]


def _combine(pos1, pos2, y, x1, mod3, route, final_g, layer, bsz, seq):
    t, d = x1.shape
    tpb = seq // TM_COMB
    grid_spec = pltpu.PrefetchScalarGridSpec(
        num_scalar_prefetch=2,
        grid=(t // TM_COMB,),
        in_specs=[
            pl.BlockSpec(memory_space=pl.ANY),
            pl.BlockSpec((TM_COMB, d), lambda i, p1, p2: (i, 0)),
            pl.BlockSpec((None, 1, d), lambda i, p1, p2: ((layer * bsz + i // tpb) * N_ADA + 5, 0, 0)),
            pl.BlockSpec((TM_COMB, LANE), lambda i, p1, p2: (i, 0)),
            pl.BlockSpec((1, d), lambda i, p1, p2: (0, 0)),
        ],
        out_specs=pl.BlockSpec((TM_COMB, d), lambda i, p1, p2: (i, 0)),
        scratch_shapes=[pltpu.VMEM((TM_COMB, d), F32), pltpu.VMEM((TM_COMB, d), F32),
                        pltpu.SemaphoreType.DMA(())],
    )
    return pl.pallas_call(
        _combine_kernel,
        grid_spec=grid_spec,
        out_shape=jax.ShapeDtypeStruct((t, d), F32),
        compiler_params=_cparams(("arbitrary",)),
        name="expert_combine_norm",
    )(pos1, pos2, y, x1, mod3, route, final_g.reshape(1, d))


def kernel(x, c, ada_w, ada_b, norm_mix_g, norm_ffn_g, w_in, s5_a_re, s5_a_im, s5_log_dt, s5_b_re, s5_b_im, s5_c_re, s5_c_im, s5_d, s5_w_glu, s5_b_glu, s5_norm_g, lru_conv_w, lru_conv_b, lru_w_a, lru_b_a, lru_w_x, lru_b_x, lru_lambda, lru_norm_g, gla_w_g2, gla_b_g2, gla_head_g, w_out, ffn_w1, ffn_w3, ffn_w2, moe_router, moe_w1, moe_w3, moe_w2, final_norm_g):
    bsz, seq, d = x.shape
    depth = ada_w.shape[0]
    assert depth == 2 and d == D_MODEL and seq % TS == 0 and (bsz * seq) % TM_FFN == 0
    t = bsz * seq
    mod3 = _adaln_mod(c, ada_w, ada_b).reshape(depth * bsz * N_ADA, 1, D_MODEL)
    xf = x.reshape(t, D_MODEL)
    out = None
    for l in range(depth):
        proj = _in_projection(xf, norm_mix_g[l], mod3, _pack_w_in(w_in[l]), l, bsz, seq)
        s5p = _s5_params(s5_a_re[l], s5_a_im[l], s5_log_dt[l], s5_b_re[l], s5_b_im[l], s5_c_re[l],
                         s5_c_im[l], s5_d[l], s5_w_glu[l], s5_b_glu[l], s5_norm_g[l])
        lrup = _lru_params(lru_conv_w[l], lru_conv_b[l], lru_w_a[l], lru_b_a[l], lru_w_x[l],
                           lru_b_x[l], lru_lambda[l], lru_norm_g[l])
        glap = _gla_params(gla_w_g2[l], gla_b_g2[l], gla_head_g[l])
        y_s5 = _s5_mixer(proj, s5p, bsz, seq)
        y_lru = _lru_mixer(proj, lrup, bsz, seq)
        y_gla = _gla_mixer(proj, glap, bsz, seq)
        w_o = _pack_w_out(w_out[l])
        if l % 2 == 0:
            x1, h = _out_projection(y_s5, y_lru, y_gla, xf, w_o, norm_ffn_g[l], mod3, l, bsz, seq)
            tables = _dense_tables(t)
            a = _ffn_gateup(tables, h, ffn_w1[l // 2:l // 2 + 1], ffn_w3[l // 2:l // 2 + 1])
            xf = _ffn_down(tables, a, ffn_w2[l // 2:l // 2 + 1], residual=(x1, mod3, l, bsz, seq))
        else:
            x1, h, logits = _out_projection(y_s5, y_lru, y_gla, xf, w_o, norm_ffn_g[l], mod3, l, bsz,
                                            seq, router=moe_router[l // 2])
            route, counts = _route(logits)
            tables, pos1, pos2, src = _expert_tables(route, counts, t)
            xs = _gather_rows(src, tables[2], tables[3], h)
            a = _ffn_gateup(tables, xs, moe_w1[l // 2], moe_w3[l // 2])
            y = _ffn_down(tables, a, moe_w2[l // 2])
            out = _combine(pos1, pos2, y, x1, mod3, route, final_norm_g, l, bsz, seq)
    return out.reshape(bsz, seq, D_MODEL)
```

```python
import functools
import math

import jax
import jax.numpy as jnp
from jax import lax
from jax.experimental import pallas as pl
from jax.experimental.pallas import tpu as pltpu

F32 = jnp.float32
BF16 = jnp.bfloat16

D_MODEL = 2048
S5_WIDTH = 512
S5_GROUP = 16
S5_GROUPS = 32
S5_STATE = 64
S5_NSTATE = S5_GROUPS * S5_STATE
LRU_WIDTH = 768
LRU_HEADS = 8
LRU_HEAD_DIM = 96
CONV_WIDTH = 4
LRU_C = 8.0
GLA_WIDTH = 768
GLA_HEADS = 4
GLA_DV = 192
GLA_DK = 96
GLA_GATE_RANK = 16
GLA_TAU = 16.0
GLA_CHUNK = 64
N_EXPERTS = 8
N_ADA = 6
EPS = 1e-6

LANE = 128
SUBLANE = 8
GLA_DKP = 128
GLA_DVP = 256
GLA_QW = GLA_HEADS * GLA_DKP
GLA_VW = GLA_HEADS * GLA_DVP
VMEM_LIMIT = 56 * 1024 * 1024

COL_LRU_X = 0
COL_LRU_G = 768
COL_S5 = 1536
COL_Q = 2048
COL_K = 2560
COL_V = 3072
COL_OG = 4096
COL_GLOW = 5120
NP_IN = 5248
NP_OUT_ROWS = S5_WIDTH + LRU_WIDTH + GLA_VW

TS = 512
NSEG = SUBLANE
SEG = TS // NSEG

TM_PROJ = 256
TM_FFN = 512
TM_DOWN = 256
TN_FFN = 512
TM_ROUTE = 512
TM_COMB = 256


def _cparams(sem, vmem=VMEM_LIMIT):
    return pltpu.CompilerParams(dimension_semantics=sem, vmem_limit_bytes=vmem)


def _dot(a, b):
    return jnp.dot(a, b, preferred_element_type=F32)


def _resident(shape):
    nd = len(shape)
    return pl.BlockSpec(shape, lambda *_: (0,) * nd, pipeline_mode=pl.Buffered(1))


def _mod_kernel(c_ref, w_ref, b_ref, o_ref):
    c = c_ref[...]
    cs = (c * jax.nn.sigmoid(c)).astype(BF16)
    o_ref[...] = _dot(cs, w_ref[...].astype(BF16)) + b_ref[...]


def _adaln_mod(c, ada_w, ada_b):
    depth, d, n = ada_w.shape
    bsz = c.shape[0]
    tn = 1024
    return pl.pallas_call(
        _mod_kernel,
        grid=(depth, n // tn),
        in_specs=[
            pl.BlockSpec((bsz, d), lambda l, j: (0, 0)),
            pl.BlockSpec((None, d, tn), lambda l, j: (l, 0, j)),
            pl.BlockSpec((None, 1, tn), lambda l, j: (l, 0, j)),
        ],
        out_specs=pl.BlockSpec((None, bsz, tn), lambda l, j: (l, 0, j)),
        out_shape=jax.ShapeDtypeStruct((depth, bsz, n), F32),
        compiler_params=_cparams(("arbitrary", "arbitrary")),
        name="adaln_mod",
    )(c, ada_w, ada_b.reshape(depth, 1, n))


def _mod_spec(layer, bsz, which, tiles_per_batch):
    def imap(i, *_):
        return ((layer * bsz + i // tiles_per_batch) * N_ADA + which, 0, 0)
    return pl.BlockSpec((None, 1, D_MODEL), imap)


def _modulated_norm(x, g, scale, shift):
    ms = jnp.mean(x * x, axis=-1, keepdims=True)
    h = x * lax.rsqrt(ms + EPS) * g
    return h * (1.0 + scale) + shift


_IN_CHUNKS = tuple((s, min(512, NP_IN - s)) for s in range(0, NP_IN, 512))


def _inproj_kernel(x_ref, g_ref, sh_ref, sc_ref, w_ref, o_ref):
    h = _modulated_norm(x_ref[...], g_ref[...], sc_ref[...], sh_ref[...]).astype(BF16)
    for s, w in _IN_CHUNKS:
        o_ref[:, s:s + w] = _dot(h, w_ref[:, s:s + w])


def _in_projection(x2, g, mod3, w_packed, layer, bsz, seq):
    t = x2.shape[0]
    tpb = seq // TM_PROJ
    return pl.pallas_call(
        _inproj_kernel,
        grid=(t // TM_PROJ,),
        in_specs=[
            pl.BlockSpec((TM_PROJ, D_MODEL), lambda i: (i, 0)),
            pl.BlockSpec((1, D_MODEL), lambda i: (0, 0)),
            _mod_spec(layer, bsz, 0, tpb),
            _mod_spec(layer, bsz, 1, tpb),
            _resident((D_MODEL, NP_IN)),
        ],
        out_specs=pl.BlockSpec((TM_PROJ, NP_IN), lambda i: (i, 0)),
        out_shape=jax.ShapeDtypeStruct((t, NP_IN), F32),
        compiler_params=_cparams(("arbitrary",)),
        name="in_projection",
    )(x2, g.reshape(1, D_MODEL), mod3, mod3, w_packed)


S5_TS = 256
S5_NROW = SUBLANE
S5_SW = S5_NSTATE // S5_NROW
S5_LT = S5_SW // LANE
S5_PITCH = S5_TS + SUBLANE
S5_NOUT = 2
S5_RPO = S5_NROW // S5_NOUT


def _s5_kernel(u_ref, bwr_ref, bwi_ref, cw_ref, lam_ref, d_ref, wg_ref, bg_ref, ng_ref, o_ref,
               slab_ref, st_ref):
    @pl.when(pl.program_id(0) == 0)
    def _():
        st_ref[...] = jnp.zeros_like(st_ref)

    nbatch = u_ref.shape[0]
    for b in range(nbatch):
        ub = u_ref[b].astype(BF16)
        for j in range(S5_NROW):
            uj = ub[:, (j // 2) * LANE:(j // 2 + 1) * LANE]
            parts = (_dot(uj, bwr_ref[j]), _dot(uj, bwi_ref[j]))
            rows = slice(j * S5_PITCH, j * S5_PITCH + S5_TS)
            for c in range(2):
                for q in range(S5_LT):
                    slab_ref[b, c, q, rows, :] = parts[c][:, q * LANE:(q + 1) * LANE]

    lr = [lam_ref[0, :, q * LANE:(q + 1) * LANE] for q in range(S5_LT)]
    li = [lam_ref[1, :, q * LANE:(q + 1) * LANE] for q in range(S5_LT)]

    def step(k, carry):
        rows = pl.ds(k, S5_NROW, stride=S5_PITCH)
        out = []
        for b in range(nbatch):
            for q in range(S5_LT):
                hr, hi = carry[b * S5_LT + q]
                nr = lr[q] * hr - li[q] * hi + slab_ref[b, 0, q, rows, :]
                ni = lr[q] * hi + li[q] * hr + slab_ref[b, 1, q, rows, :]
                slab_ref[b, 0, q, rows, :] = nr
                slab_ref[b, 1, q, rows, :] = ni
                out.append((nr, ni))
        return tuple(out)

    init = tuple((st_ref[b, 0, q], st_ref[b, 1, q]) for b in range(nbatch) for q in range(S5_LT))
    final = lax.fori_loop(0, S5_TS, step, init, unroll=2)
    for b in range(nbatch):
        for q in range(S5_LT):
            st_ref[b, 0, q] = final[b * S5_LT + q][0]
            st_ref[b, 1, q] = final[b * S5_LT + q][1]

    for b in range(nbatch):
        ys = []
        for p in range(S5_NOUT):
            cols = []
            for c in range(2):
                for j in range(p * S5_RPO, (p + 1) * S5_RPO):
                    rows = slice(j * S5_PITCH, j * S5_PITCH + S5_TS)
                    cols += [slab_ref[b, c, q, rows, :] for q in range(S5_LT)]
            ys.append(_dot(jnp.concatenate(cols, axis=1).astype(BF16), cw_ref[p]))
        y = jnp.concatenate(ys, axis=1) + d_ref[...] * u_ref[b]
        z = jax.nn.gelu(y).astype(BF16)
        zz = _dot(z, wg_ref[...]) + bg_ref[...]
        o = zz[:, :S5_WIDTH] * jax.nn.sigmoid(zz[:, S5_WIDTH:])
        o = o * lax.rsqrt(jnp.mean(o * o, axis=-1, keepdims=True) + EPS) * ng_ref[...]
        o_ref[b] = o.astype(BF16)


def _s5_mixer(proj, p, bsz, seq):
    t = bsz * seq
    ublk = COL_S5 // S5_WIDTH
    out = pl.pallas_call(
        _s5_kernel,
        grid=(seq // S5_TS,),
        in_specs=[
            pl.BlockSpec((bsz, S5_TS, S5_WIDTH), lambda s: (0, s, ublk)),
            _resident((S5_NROW, LANE, S5_SW)),
            _resident((S5_NROW, LANE, S5_SW)),
            _resident((S5_NOUT, 2 * S5_RPO * S5_SW, S5_WIDTH // S5_NOUT)),
            _resident((2, S5_NROW, S5_SW)),
            _resident((1, S5_WIDTH)),
            _resident((S5_WIDTH, 2 * S5_WIDTH)),
            _resident((1, 2 * S5_WIDTH)),
            _resident((1, S5_WIDTH)),
        ],
        out_specs=pl.BlockSpec((bsz, S5_TS, S5_WIDTH), lambda s: (0, s, 0)),
        out_shape=jax.ShapeDtypeStruct((bsz, seq, S5_WIDTH), BF16),
        scratch_shapes=[
            pltpu.VMEM((bsz, 2, S5_LT, S5_NROW * S5_PITCH, LANE), F32),
            pltpu.VMEM((bsz, 2, S5_LT, S5_NROW, LANE), F32),
        ],
        compiler_params=_cparams(("arbitrary",)),
        name="s5_mixer",
    )(proj.reshape(bsz, seq, -1), p["bw_re"], p["bw_im"], p["cw"], p["lam"],
      p["d"], p["w_glu"], p["b_glu"], p["norm_g"])
    return out.reshape(t, S5_WIDTH)


def _s5_params(a_re, a_im, log_dt, b_re, b_im, c_re, c_im, d_skip, w_glu, b_glu, norm_g):
    dt = jnp.exp(log_dt)[:, None]
    mag = jnp.exp(a_re * dt)
    abar_re = mag * jnp.cos(a_im * dt)
    abar_im = mag * jnp.sin(a_im * dt)
    nr = abar_re - 1.0
    ni = abar_im
    den = a_re * a_re + a_im * a_im
    f_re = (nr * a_re + ni * a_im) / den
    f_im = (ni * a_re - nr * a_im) / den
    bbar_re = f_re[..., None] * b_re - f_im[..., None] * b_im
    bbar_im = f_re[..., None] * b_im + f_im[..., None] * b_re
    eye = jnp.eye(S5_GROUPS, dtype=F32)
    bbd = lambda m: jnp.einsum("gpc,gh->gchp", m, eye).reshape(S5_WIDTH, S5_NSTATE)
    cbd = lambda m: jnp.einsum("gcp,gh->gphc", m, eye).reshape(S5_NSTATE, S5_WIDTH)
    bw = lambda m: jnp.stack([m[(j // 2) * LANE:(j // 2 + 1) * LANE, j * S5_SW:(j + 1) * S5_SW]
                              for j in range(S5_NROW)]).astype(BF16)
    cre, cim = cbd(c_re), cbd(c_im)
    ns, nc = S5_RPO * S5_SW, S5_WIDTH // S5_NOUT
    cw = jnp.stack([jnp.concatenate([cre[p * ns:(p + 1) * ns, p * nc:(p + 1) * nc],
                                     -cim[p * ns:(p + 1) * ns, p * nc:(p + 1) * nc]], axis=0)
                    for p in range(S5_NOUT)]).astype(BF16)
    lam = jnp.stack([abar_re.reshape(S5_NROW, S5_SW), abar_im.reshape(S5_NROW, S5_SW)])
    return {
        "bw_re": bw(bbd(bbar_re)), "bw_im": bw(bbd(bbar_im)), "cw": cw, "lam": lam,
        "d": d_skip.reshape(1, S5_WIDTH), "w_glu": w_glu.astype(BF16),
        "b_glu": b_glu.reshape(1, -1), "norm_g": norm_g.reshape(1, S5_WIDTH),
    }


LRU_LT = LRU_WIDTH // LANE
LRU_PITCH = SEG + SUBLANE


def _lru_kernel(x_ref, gb_ref, cw_ref, cb_ref, wa_ref, ba_ref, wx_ref, bx_ref, lam_ref, ng_ref,
                o_ref, xe_ref, a_ref, b_ref, hc_ref):
    first = pl.program_id(1) == 0

    @pl.when(first)
    def _():
        xe_ref[0:SUBLANE, :] = jnp.zeros((SUBLANE, LRU_WIDTH), F32)
        hc_ref[...] = jnp.zeros_like(hc_ref)

    @pl.when(jnp.logical_not(first))
    def _():
        xe_ref[0:SUBLANE, :] = xe_ref[TS:TS + SUBLANE, :]

    xe_ref[SUBLANE:TS + SUBLANE, :] = x_ref[...]
    xc = cb_ref[...]
    for w in range(CONV_WIDTH):
        off = SUBLANE - (CONV_WIDTH - 1) + w
        xc = xc + cw_ref[w:w + 1, :] * xe_ref[off:off + TS, :]
    xb = xc.astype(BF16)
    r = jax.nn.sigmoid(_dot(xb, wa_ref[...]) + ba_ref[...])
    gi = jax.nn.sigmoid(_dot(xb, wx_ref[...]) + bx_ref[...])
    log_a = LRU_C * r * jax.nn.log_sigmoid(lam_ref[...])
    a = jnp.exp(log_a)
    one_minus_a2 = -jnp.tanh(log_a) * (1.0 + a * a)
    bt = jnp.sqrt(jnp.maximum(one_minus_a2, 0.0)) * gi * xc
    for j in range(NSEG):
        src = slice(j * SEG, (j + 1) * SEG)
        dst = slice(j * LRU_PITCH, j * LRU_PITCH + SEG)
        for q in range(LRU_LT):
            a_ref[q, dst, :] = a[src, q * LANE:(q + 1) * LANE]
            b_ref[q, dst, :] = bt[src, q * LANE:(q + 1) * LANE]

    def local_scan(k, carry):
        rows = pl.ds(k, NSEG, stride=LRU_PITCH)
        out = []
        for q in range(LRU_LT):
            h, ap = carry[q]
            aq = a_ref[q, rows, :]
            h = aq * h + b_ref[q, rows, :]
            ap = ap * aq
            b_ref[q, rows, :] = h
            a_ref[q, rows, :] = ap
            out.append((h, ap))
        return tuple(out)

    init = (jnp.zeros((NSEG, LANE), F32), jnp.ones((NSEG, LANE), F32))
    ends = lax.fori_loop(0, SEG, local_scan, tuple(init for _ in range(LRU_LT)))
    e = jnp.concatenate([ends[q][0] for q in range(LRU_LT)], axis=1)
    ap = jnp.concatenate([ends[q][1] for q in range(LRU_LT)], axis=1)

    s = hc_ref[...]
    g = gb_ref[...]
    for j in range(NSEG):
        rs = slice(j * SEG, (j + 1) * SEG)
        ps = slice(j * LRU_PITCH, j * LRU_PITCH + SEG)
        hl = jnp.concatenate([b_ref[q, ps, :] for q in range(LRU_LT)], axis=1)
        al = jnp.concatenate([a_ref[q, ps, :] for q in range(LRU_LT)], axis=1)
        h = hl + al * s
        o = h * jax.nn.gelu(g[rs, :])
        o = o * lax.rsqrt(jnp.mean(o * o, axis=-1, keepdims=True) + EPS) * ng_ref[...]
        o_ref[rs, :] = o.astype(BF16)
        s = e[j:j + 1] + ap[j:j + 1] * s
    hc_ref[...] = s


def _lru_mixer(proj, p, bsz, seq):
    t = bsz * seq
    nb = seq // TS
    vec = _resident((1, LRU_WIDTH))
    return pl.pallas_call(
        _lru_kernel,
        grid=(bsz, nb),
        in_specs=[
            pl.BlockSpec((TS, LRU_WIDTH), lambda b, s: (b * nb + s, COL_LRU_X // LRU_WIDTH)),
            pl.BlockSpec((TS, LRU_WIDTH), lambda b, s: (b * nb + s, COL_LRU_G // LRU_WIDTH)),
            _resident((CONV_WIDTH, LRU_WIDTH)), vec,
            _resident((LRU_WIDTH, LRU_WIDTH)), vec,
            _resident((LRU_WIDTH, LRU_WIDTH)), vec,
            vec, vec,
        ],
        out_specs=pl.BlockSpec((TS, LRU_WIDTH), lambda b, s: (b * nb + s, 0)),
        out_shape=jax.ShapeDtypeStruct((t, LRU_WIDTH), BF16),
        scratch_shapes=[
            pltpu.VMEM((TS + SUBLANE, LRU_WIDTH), F32),
            pltpu.VMEM((LRU_LT, NSEG * LRU_PITCH, LANE), F32),
            pltpu.VMEM((LRU_LT, NSEG * LRU_PITCH, LANE), F32),
            pltpu.VMEM((1, LRU_WIDTH), F32),
        ],
        compiler_params=_cparams(("arbitrary", "arbitrary")),
        name="rglru_mixer",
    )(proj, proj, p["conv_w"], p["conv_b"], p["wa"], p["ba"], p["wx"], p["bx"], p["lam"], p["norm_g"])


def _lru_params(conv_w, conv_b, w_a, b_a, w_x, b_x, lam, norm_g):
    eye = jnp.eye(LRU_HEADS, dtype=F32)
    bd = lambda m: jnp.einsum("hij,hk->hikj", m, eye).reshape(LRU_WIDTH, LRU_WIDTH).astype(BF16)
    row = lambda v: v.reshape(1, LRU_WIDTH)
    return {"conv_w": conv_w, "conv_b": row(conv_b), "wa": bd(w_a), "ba": row(b_a),
            "wx": bd(w_x), "bx": row(b_x), "lam": row(lam), "norm_g": row(norm_g)}


def _split3(x):
    hi = x.astype(BF16)
    r1 = x - hi.astype(F32)
    mid = r1.astype(BF16)
    lo = (r1 - mid.astype(F32)).astype(BF16)
    return hi, mid, lo


def _gla_kernel(q_ref, k_ref, v_ref, og_ref, gl_ref, wg_ref, bg_ref, hg_ref, o_ref, st_ref):
    @pl.when(pl.program_id(1) == 0)
    def _():
        st_ref[...] = jnp.zeros_like(st_ref)

    c = GLA_CHUNK
    hc = GLA_HEADS * c
    ri = lax.broadcasted_iota(jnp.int32, (c, c), 0)
    ci = lax.broadcasted_iota(jnp.int32, (c, c), 1)
    tri = (ri >= ci).astype(BF16)
    rr = lax.broadcasted_iota(jnp.int32, (hc, hc), 0)
    cc = lax.broadcasted_iota(jnp.int32, (hc, hc), 1)
    causal = jnp.logical_and(rr // c == cc // c, rr % c >= cc % c)
    heads = lambda m, w: jnp.concatenate([m[:, h * w:(h + 1) * w] for h in range(GLA_HEADS)], axis=0)

    for n in range(TS // c):
        rs = slice(n * c, (n + 1) * c)
        logit = _dot(gl_ref[rs, :].astype(BF16), wg_ref[...]) + bg_ref[...]
        la = jax.nn.log_sigmoid(logit) / GLA_TAU
        hi, mid, lo = _split3(la)
        bcum = _dot(tri, hi) + _dot(tri, mid) + _dot(tri, lo)
        b_last = bcum[c - 1:c, :]
        kk = k_ref[rs, :]
        q_dec = q_ref[rs, :] * (GLA_DK ** -0.5) * jnp.exp(bcum)
        k_in = kk * jnp.exp(-bcum)
        k_out = (kk * jnp.exp(b_last - bcum)).astype(BF16)
        gamma = jnp.exp(b_last)
        vb = v_ref[rs, :].astype(BF16)
        qd = q_dec.astype(BF16)

        att = lax.dot_general(heads(qd, GLA_DKP), heads(k_in.astype(BF16), GLA_DKP),
                              (((1,), (1,)), ((), ())), preferred_element_type=F32)
        att = jnp.where(causal, att, 0.0).astype(BF16)
        o_s = _dot(att, heads(vb, GLA_DVP))
        outs = []
        for h in range(GLA_HEADS):
            ks = slice(h * GLA_DKP, (h + 1) * GLA_DKP)
            vs = slice(h * GLA_DVP, (h + 1) * GLA_DVP)
            st = st_ref[h]
            o_h = o_s[h * c:(h + 1) * c, :] + lax.dot_general(
                qd[:, ks], st.astype(BF16), (((1,), (1,)), ((), ())), preferred_element_type=F32)
            delta = lax.dot_general(vb[:, vs], k_out[:, ks], (((0,), (0,)), ((), ())),
                                    preferred_element_type=F32)
            st_ref[h] = gamma[:, ks] * st + delta
            ms = jnp.sum(o_h * o_h, axis=-1, keepdims=True) * (1.0 / GLA_DV)
            outs.append(o_h * lax.rsqrt(ms + EPS) * hg_ref[...])
        o = jnp.concatenate(outs, axis=1)
        og = og_ref[rs, :]
        o_ref[rs, :] = (o * (og * jax.nn.sigmoid(og))).astype(BF16)


def _gla_mixer(proj, p, bsz, seq):
    t = bsz * seq
    nb = seq // TS
    row = lambda b, s: b * nb + s
    return pl.pallas_call(
        _gla_kernel,
        grid=(bsz, nb),
        in_specs=[
            pl.BlockSpec((TS, GLA_QW), lambda b, s: (row(b, s), COL_Q // GLA_QW)),
            pl.BlockSpec((TS, GLA_QW), lambda b, s: (row(b, s), COL_K // GLA_QW)),
            pl.BlockSpec((TS, GLA_VW), lambda b, s: (row(b, s), COL_V // GLA_VW)),
            pl.BlockSpec((TS, GLA_VW), lambda b, s: (row(b, s), COL_OG // GLA_VW)),
            pl.BlockSpec((TS, LANE), lambda b, s: (row(b, s), COL_GLOW // LANE)),
            _resident((LANE, GLA_QW)),
            _resident((1, GLA_QW)),
            _resident((1, GLA_DVP)),
        ],
        out_specs=pl.BlockSpec((TS, GLA_VW), lambda b, s: (row(b, s), 0)),
        out_shape=jax.ShapeDtypeStruct((t, GLA_VW), BF16),
        scratch_shapes=[pltpu.VMEM((GLA_HEADS, GLA_DVP, GLA_DKP), F32)],
        compiler_params=_cparams(("arbitrary", "arbitrary")),
        name="gla_mixer",
    )(proj, proj, proj, proj, proj, p["w_g2"], p["b_g2"], p["head_g"])


def _pad_heads(m, d, dp):
    lead = m.shape[:-1]
    m = m.reshape(lead + (GLA_HEADS, d))
    m = jnp.pad(m, [(0, 0)] * len(lead) + [(0, 0), (0, dp - d)])
    return m.reshape(lead + (GLA_HEADS * dp,))


def _gla_params(w_g2, b_g2, head_g):
    w = jnp.pad(_pad_heads(w_g2, GLA_DK, GLA_DKP), ((0, LANE - GLA_GATE_RANK), (0, 0)))
    return {"w_g2": w.astype(BF16),
            "b_g2": _pad_heads(b_g2, GLA_DK, GLA_DKP).reshape(1, GLA_QW),
            "head_g": jnp.pad(head_g, (0, GLA_DVP - GLA_DV)).reshape(1, GLA_DVP)}


def _pack_w_in(w):
    s5 = w[:, 0:512]
    lx = w[:, 512:1280]
    lg = w[:, 1280:2048]
    q = w[:, 2048:2432]
    k = w[:, 2432:2816]
    v = w[:, 2816:3584]
    gl = w[:, 3584:3600]
    og = w[:, 3600:4368]
    cols = [lx, lg, s5, _pad_heads(q, GLA_DK, GLA_DKP), _pad_heads(k, GLA_DK, GLA_DKP),
            _pad_heads(v, GLA_DV, GLA_DVP), _pad_heads(og, GLA_DV, GLA_DVP),
            jnp.pad(gl, ((0, 0), (0, LANE - GLA_GATE_RANK)))]
    return jnp.concatenate(cols, axis=1).astype(BF16)


def _pack_w_out(w):
    gla = w[S5_WIDTH + LRU_WIDTH:].reshape(GLA_HEADS, GLA_DV, D_MODEL)
    gla = jnp.pad(gla, ((0, 0), (0, GLA_DVP - GLA_DV), (0, 0))).reshape(GLA_VW, D_MODEL)
    return jnp.concatenate([w[:S5_WIDTH + LRU_WIDTH], gla], axis=0).astype(BF16)


def _outproj_kernel(ys_ref, yl_ref, yg_ref, x_ref, w_ref, gate_ref, g_ref, sh_ref, sc_ref, *rest,
                    with_router):
    r0, r1 = S5_WIDTH, S5_WIDTH + LRU_WIDTH
    mix = (_dot(ys_ref[...], w_ref[0:r0, :]) + _dot(yl_ref[...], w_ref[r0:r1, :])
           + _dot(yg_ref[...], w_ref[r1:NP_OUT_ROWS, :]))
    x1 = x_ref[...] + gate_ref[...] * mix
    h = _modulated_norm(x1, g_ref[...], sc_ref[...], sh_ref[...])
    if with_router:
        wr_hi_ref, wr_lo_ref, x1_ref, h_ref, lg_ref = rest
        h_ref[...] = h
        hh = h.astype(BF16)
        hm = (h - hh.astype(F32)).astype(BF16)
        lg_ref[...] = (_dot(hh, wr_hi_ref[...]) + _dot(hm, wr_hi_ref[...])
                       + _dot(hh, wr_lo_ref[...]))
    else:
        x1_ref, h_ref = rest
        h_ref[...] = h.astype(BF16)
    x1_ref[...] = x1


def _out_projection(ys, yl, yg, x2, w_packed, g, mod3, layer, bsz, seq, router=None):
    t = x2.shape[0]
    tpb = seq // TM_PROJ
    rowblk = lambda w: pl.BlockSpec((TM_PROJ, w), lambda i: (i, 0))
    in_specs = [rowblk(S5_WIDTH), rowblk(LRU_WIDTH), rowblk(GLA_VW), rowblk(D_MODEL),
                _resident((NP_OUT_ROWS, D_MODEL)),
                _mod_spec(layer, bsz, 2, tpb),
                pl.BlockSpec((1, D_MODEL), lambda i: (0, 0)),
                _mod_spec(layer, bsz, 3, tpb), _mod_spec(layer, bsz, 4, tpb)]
    args = [ys, yl, yg, x2, w_packed, mod3, g.reshape(1, D_MODEL), mod3, mod3]
    if router is None:
        out_specs = [rowblk(D_MODEL), rowblk(D_MODEL)]
        out_shape = [jax.ShapeDtypeStruct((t, D_MODEL), F32), jax.ShapeDtypeStruct((t, D_MODEL), BF16)]
    else:
        wr = jnp.pad(router, ((0, 0), (0, LANE - N_EXPERTS)))
        wr_hi = wr.astype(BF16)
        wr_lo = (wr - wr_hi.astype(F32)).astype(BF16)
        in_specs += [_resident((D_MODEL, LANE)), _resident((D_MODEL, LANE))]
        args += [wr_hi, wr_lo]
        out_specs = [rowblk(D_MODEL), rowblk(D_MODEL), rowblk(LANE)]
        out_shape = [jax.ShapeDtypeStruct((t, D_MODEL), F32), jax.ShapeDtypeStruct((t, D_MODEL), F32),
                     jax.ShapeDtypeStruct((t, LANE), F32)]
    return pl.pallas_call(
        functools.partial(_outproj_kernel, with_router=router is not None),
        grid=(t // TM_PROJ,),
        in_specs=in_specs, out_specs=out_specs, out_shape=out_shape,
        compiler_params=_cparams(("arbitrary",)),
        name="out_projection",
    )(*args)


def _gateup_kernel(te_ref, tf_ref, tv_ref, ti_ref, x_ref, w1_ref, w3_ref, o_ref, w1b_ref, w3b_ref):
    i = pl.program_id(1)

    @pl.when(tf_ref[i] == 1)
    def _():
        w1b_ref[...] = w1_ref[...].astype(BF16)
        w3b_ref[...] = w3_ref[...].astype(BF16)

    @pl.when(tv_ref[i] == 1)
    def _():
        x = x_ref[...]
        h1 = _dot(x, w1b_ref[...])
        h3 = _dot(x, w3b_ref[...])
        o_ref[...] = (h1 * jax.nn.sigmoid(h1) * h3).astype(BF16)

    @pl.when(tv_ref[i] == 0)
    def _():
        o_ref[...] = jnp.zeros_like(o_ref)


def _ffn_gateup(tables, xs, w1, w3):
    r, d = xs.shape
    n_e, _, dff = w1.shape
    nt = r // TM_FFN
    grid_spec = pltpu.PrefetchScalarGridSpec(
        num_scalar_prefetch=4,
        grid=(dff // TN_FFN, nt),
        in_specs=[
            pl.BlockSpec((TM_FFN, d), lambda j, i, te, tf, tv, ti: (ti[i], 0)),
            pl.BlockSpec((None, d, TN_FFN), lambda j, i, te, tf, tv, ti: (te[i], 0, j)),
            pl.BlockSpec((None, d, TN_FFN), lambda j, i, te, tf, tv, ti: (te[i], 0, j)),
        ],
        out_specs=pl.BlockSpec((TM_FFN, TN_FFN), lambda j, i, te, tf, tv, ti: (i, j)),
        scratch_shapes=[pltpu.VMEM((d, TN_FFN), BF16), pltpu.VMEM((d, TN_FFN), BF16)],
    )
    return pl.pallas_call(
        _gateup_kernel,
        grid_spec=grid_spec,
        out_shape=jax.ShapeDtypeStruct((r, dff), BF16),
        compiler_params=_cparams(("arbitrary", "arbitrary")),
        name="ffn_gate_up",
    )(*tables, xs, w1, w3)


def _down_kernel(te_ref, tf_ref, tv_ref, ti_ref, a_ref, w2_ref, *rest, residual):
    i = pl.program_id(1)
    sub = TM_FFN // TM_DOWN
    if residual:
        x1_ref, gate_ref, o_ref, wb_ref = rest
    else:
        o_ref, wb_ref = rest

    @pl.when(jnp.logical_and(tf_ref[i // sub] == 1, i % sub == 0))
    def _():
        wb_ref[...] = w2_ref[...].astype(BF16)

    @pl.when(tv_ref[i // sub] == 1)
    def _():
        y = _dot(a_ref[...], wb_ref[...])
        if residual:
            y = x1_ref[...] + gate_ref[...] * y
        o_ref[...] = y

    @pl.when(tv_ref[i // sub] == 0)
    def _():
        o_ref[...] = jnp.zeros_like(o_ref)


def _ffn_down(tables, a, w2, residual=None):
    r, dff = a.shape
    d = w2.shape[2]
    sub = TM_FFN // TM_DOWN
    nt = r // TM_DOWN

    def rowidx(i, ti, tv):
        return ti[i // sub] * sub + jnp.where(tv[i // sub] == 1, i % sub, sub - 1)

    in_specs = [
        pl.BlockSpec((TM_DOWN, dff), lambda j, i, te, tf, tv, ti: (rowidx(i, ti, tv), 0)),
        pl.BlockSpec((None, dff, TN_FFN), lambda j, i, te, tf, tv, ti: (te[i // sub], 0, j)),
    ]
    args = [a, w2]
    if residual is not None:
        x1, mod3, layer, bsz, seq = residual
        tpb = seq // TM_DOWN
        in_specs += [
            pl.BlockSpec((TM_DOWN, TN_FFN), lambda j, i, te, tf, tv, ti: (i, j)),
            pl.BlockSpec((None, 1, TN_FFN),
                         lambda j, i, te, tf, tv, ti: ((layer * bsz + i // tpb) * N_ADA + 5, 0, j)),
        ]
        args += [x1, mod3]
    grid_spec = pltpu.PrefetchScalarGridSpec(
        num_scalar_prefetch=4,
        grid=(d // TN_FFN, nt),
        in_specs=in_specs,
        out_specs=pl.BlockSpec((TM_DOWN, TN_FFN), lambda j, i, te, tf, tv, ti: (i, j)),
        scratch_shapes=[pltpu.VMEM((dff, TN_FFN), BF16)],
    )
    return pl.pallas_call(
        functools.partial(_down_kernel, residual=residual is not None),
        grid_spec=grid_spec,
        out_shape=jax.ShapeDtypeStruct((r, d), F32),
        compiler_params=_cparams(("arbitrary", "arbitrary")),
        name="ffn_down",
    )(*tables, *args)


def _dense_tables(t):
    nt = t // TM_FFN
    ids = jnp.arange(nt, dtype=jnp.int32)
    return (jnp.zeros((nt,), jnp.int32), (ids == 0).astype(jnp.int32),
            jnp.ones((nt,), jnp.int32), ids)


def _route_kernel(lg_ref, o_ref, cnt_ref, carry_ref):
    @pl.when(pl.program_id(0) == 0)
    def _():
        carry_ref[...] = jnp.zeros_like(carry_ref)

    tm = TM_ROUTE
    lane = lax.broadcasted_iota(jnp.int32, (tm, LANE), 1)
    neg = jnp.float32(-jnp.inf)
    l = jnp.where(lane < N_EXPERTS, lg_ref[...], neg)
    m1 = jnp.max(l, axis=1, keepdims=True)
    i1 = jnp.min(jnp.where(l == m1, lane, LANE), axis=1, keepdims=True)
    oh1 = lane == i1
    l2 = jnp.where(oh1, neg, l)
    m2 = jnp.max(l2, axis=1, keepdims=True)
    i2 = jnp.min(jnp.where(l2 == m2, lane, LANE), axis=1, keepdims=True)
    oh2 = lane == i2
    e = jnp.exp(m2 - m1)
    den = 1.0 + e
    g1 = 1.0 / den
    g2 = e / den
    oh = jnp.where(jnp.logical_or(oh1, oh2), 1.0, 0.0)
    ri = lax.broadcasted_iota(jnp.int32, (tm, tm), 0)
    ci = lax.broadcasted_iota(jnp.int32, (tm, tm), 1)
    before = (ri > ci).astype(BF16)
    cum = _dot(before, oh.astype(BF16)) + carry_ref[...]
    r1 = jnp.sum(jnp.where(oh1, cum, 0.0), axis=1, keepdims=True)
    r2 = jnp.sum(jnp.where(oh2, cum, 0.0), axis=1, keepdims=True)
    carry_ref[...] = carry_ref[...] + jnp.sum(oh, axis=0, keepdims=True)
    cnt_ref[...] = carry_ref[...]
    out = jnp.where(lane == 0, i1.astype(F32), 0.0)
    out = jnp.where(lane == 1, i2.astype(F32), out)
    out = jnp.where(lane == 2, g1, out)
    out = jnp.where(lane == 3, g2, out)
    out = jnp.where(lane == 4, r1, out)
    out = jnp.where(lane == 5, r2, out)
    o_ref[...] = out


def _route(logits):
    t = logits.shape[0]
    return pl.pallas_call(
        _route_kernel,
        grid=(t // TM_ROUTE,),
        in_specs=[pl.BlockSpec((TM_ROUTE, LANE), lambda i: (i, 0))],
        out_specs=[pl.BlockSpec((TM_ROUTE, LANE), lambda i: (i, 0)),
                   pl.BlockSpec((1, LANE), lambda i: (0, 0))],
        out_shape=[jax.ShapeDtypeStruct((t, LANE), F32), jax.ShapeDtypeStruct((1, LANE), F32)],
        scratch_shapes=[pltpu.VMEM((1, LANE), F32)],
        compiler_params=_cparams(("arbitrary",)),
        name="route_top2",
    )(logits)


def _expert_tables(route, counts, t):
    nt = (2 * t) // TM_FFN + N_EXPERTS
    i1 = route[:, 0].astype(jnp.int32)
    i2 = route[:, 1].astype(jnp.int32)
    r1 = route[:, 4].astype(jnp.int32)
    r2 = route[:, 5].astype(jnp.int32)
    cnt = counts[0, :N_EXPERTS].astype(jnp.int32)
    tiles = (cnt + TM_FFN - 1) // TM_FFN
    tile_end = jnp.cumsum(tiles)
    row_off = (tile_end - tiles) * TM_FFN
    n_valid = tile_end[-1]
    ids = jnp.arange(nt, dtype=jnp.int32)
    ti = jnp.minimum(ids, n_valid - 1)
    te = jnp.sum((ti[:, None] >= tile_end[None, :]).astype(jnp.int32), axis=1)
    te = jnp.minimum(te, N_EXPERTS - 1)
    tv = (ids < n_valid).astype(jnp.int32)
    prev = jnp.concatenate([jnp.full((1,), -1, jnp.int32), te[:-1]])
    tf = jnp.logical_and(tv == 1, te != prev).astype(jnp.int32)
    pos1 = row_off[i1] + r1
    pos2 = row_off[i2] + r2
    tok = jnp.arange(t, dtype=jnp.int32)
    src = jnp.zeros((nt * TM_FFN,), jnp.int32).at[pos1].set(tok).at[pos2].set(tok)
    return (te, tf, tv, ti), pos1, pos2, src


def _row_copy(src_hbm, row, dst, slot, sem):
    return pltpu.make_async_copy(src_hbm.at[pl.ds(row, 1), :], dst.at[pl.ds(slot, 1), :], sem)


def _gather_kernel(src_ref, tv_ref, ti_ref, h_hbm, o_ref, buf_ref, sem):
    i = pl.program_id(0)

    @pl.when(tv_ref[i] == 1)
    def _():
        base = i * TM_FFN

        def issue(r, c):
            _row_copy(h_hbm, src_ref[base + r], buf_ref, r, sem).start()
            return c

        lax.fori_loop(0, TM_FFN, issue, 0)

        def drain(r, c):
            _row_copy(h_hbm, 0, buf_ref, r, sem).wait()
            return c

        lax.fori_loop(0, TM_FFN, drain, 0)
        o_ref[...] = buf_ref[...].astype(BF16)

    @pl.when(tv_ref[i] == 0)
    def _():
        o_ref[...] = jnp.zeros_like(o_ref)


def _gather_rows(src, tv, ti, h):
    nt = tv.shape[0]
    d = h.shape[1]
    grid_spec = pltpu.PrefetchScalarGridSpec(
        num_scalar_prefetch=3,
        grid=(nt,),
        in_specs=[pl.BlockSpec(memory_space=pl.ANY)],
        out_specs=pl.BlockSpec((TM_FFN, d), lambda i, src, tv, ti: (i, 0)),
        scratch_shapes=[pltpu.VMEM((TM_FFN, d), F32), pltpu.SemaphoreType.DMA(())],
    )
    return pl.pallas_call(
        _gather_kernel,
        grid_spec=grid_spec,
        out_shape=jax.ShapeDtypeStruct((nt * TM_FFN, d), BF16),
        compiler_params=_cparams(("arbitrary",)),
        name="expert_gather",
    )(src, tv, ti, h)


def _combine_kernel(p1_ref, p2_ref, y_hbm, x1_ref, gate_ref, rt_ref, fg_ref, o_ref, ya_ref, yb_ref, sem):
    base = pl.program_id(0) * TM_COMB

    def issue(r, c):
        _row_copy(y_hbm, p1_ref[base + r], ya_ref, r, sem).start()
        _row_copy(y_hbm, p2_ref[base + r], yb_ref, r, sem).start()
        return c

    lax.fori_loop(0, TM_COMB, issue, 0)

    def drain(r, c):
        _row_copy(y_hbm, 0, ya_ref, r, sem).wait()
        _row_copy(y_hbm, 0, yb_ref, r, sem).wait()
        return c

    lax.fori_loop(0, TM_COMB, drain, 0)
    rt = rt_ref[...]
    f = rt[:, 2:3] * ya_ref[...] + rt[:, 3:4] * yb_ref[...]
    x2 = x1_ref[...] + gate_ref[...] * f
    o_ref[...] = x2 * lax.rsqrt(jnp.mean(x2 * x2, axis=-1, keepdims=True) + EPS) * fg_ref[...]


def _combine(pos1, pos2, y, x1, mod3, route, final_g, layer, bsz, seq):
    t, d = x1.shape
    tpb = seq // TM_COMB
    grid_spec = pltpu.PrefetchScalarGridSpec(
        num_scalar_prefetch=2,
        grid=(t // TM_COMB,),
        in_specs=[
            pl.BlockSpec(memory_space=pl.ANY),
            pl.BlockSpec((TM_COMB, d), lambda i, p1, p2: (i, 0)),
            pl.BlockSpec((None, 1, d), lambda i, p1, p2: ((layer * bsz + i // tpb) * N_ADA + 5, 0, 0)),
            pl.BlockSpec((TM_COMB, LANE), lambda i, p1, p2: (i, 0)),
            pl.BlockSpec((1, d), lambda i, p1, p2: (0, 0)),
        ],
        out_specs=pl.BlockSpec((TM_COMB, d), lambda i, p1, p2: (i, 0)),
        scratch_shapes=[pltpu.VMEM((TM_COMB, d), F32), pltpu.VMEM((TM_COMB, d), F32),
                        pltpu.SemaphoreType.DMA(())],
    )
    return pl.pallas_call(
        _combine_kernel,
        grid_spec=grid_spec,
        out_shape=jax.ShapeDtypeStruct((t, d), F32),
        compiler_params=_cparams(("arbitrary",)),
        name="expert_combine_norm",
    )(pos1, pos2, y, x1, mod3, route, final_g.reshape(1, d))


def kernel(x, c, ada_w, ada_b, norm_mix_g, norm_ffn_g, w_in, s5_a_re, s5_a_im, s5_log_dt, s5_b_re, s5_b_im, s5_c_re, s5_c_im, s5_d, s5_w_glu, s5_b_glu, s5_norm_g, lru_conv_w, lru_conv_b, lru_w_a, lru_b_a, lru_w_x, lru_b_x, lru_lambda, lru_norm_g, gla_w_g2, gla_b_g2, gla_head_g, w_out, ffn_w1, ffn_w3, ffn_w2, moe_router, moe_w1, moe_w3, moe_w2, final_norm_g):
    bsz, seq, d = x.shape
    depth = ada_w.shape[0]
    assert depth == 2 and d == D_MODEL and seq % TS == 0 and (bsz * seq) % TM_FFN == 0
    t = bsz * seq
    mod3 = _adaln_mod(c, ada_w, ada_b).reshape(depth * bsz * N_ADA, 1, D_MODEL)
    xf = x.reshape(t, D_MODEL)
    out = None
    for l in range(depth):
        proj = _in_projection(xf, norm_mix_g[l], mod3, _pack_w_in(w_in[l]), l, bsz, seq)
        s5p = _s5_params(s5_a_re[l], s5_a_im[l], s5_log_dt[l], s5_b_re[l], s5_b_im[l], s5_c_re[l],
                         s5_c_im[l], s5_d[l], s5_w_glu[l], s5_b_glu[l], s5_norm_g[l])
        lrup = _lru_params(lru_conv_w[l], lru_conv_b[l], lru_w_a[l], lru_b_a[l], lru_w_x[l],
                           lru_b_x[l], lru_lambda[l], lru_norm_g[l])
        glap = _gla_params(gla_w_g2[l], gla_b_g2[l], gla_head_g[l])
        y_s5 = _s5_mixer(proj, s5p, bsz, seq)
        y_lru = _lru_mixer(proj, lrup, bsz, seq)
        y_gla = _gla_mixer(proj, glap, bsz, seq)
        w_o = _pack_w_out(w_out[l])
        if l % 2 == 0:
            x1, h = _out_projection(y_s5, y_lru, y_gla, xf, w_o, norm_ffn_g[l], mod3, l, bsz, seq)
            tables = _dense_tables(t)
            a = _ffn_gateup(tables, h, ffn_w1[l // 2:l // 2 + 1], ffn_w3[l // 2:l // 2 + 1])
            xf = _ffn_down(tables, a, ffn_w2[l // 2:l // 2 + 1], residual=(x1, mod3, l, bsz, seq))
        else:
            x1, h, logits = _out_projection(y_s5, y_lru, y_gla, xf, w_o, norm_ffn_g[l], mod3, l, bsz,
                                            seq, router=moe_router[l // 2])
            route, counts = _route(logits)
            tables, pos1, pos2, src = _expert_tables(route, counts, t)
            xs = _gather_rows(src, tables[2], tables[3], h)
            a = _ffn_gateup(tables, xs, moe_w1[l // 2], moe_w3[l // 2])
            y = _ffn_down(tables, a, moe_w2[l // 2])
            out = _combine(pos1, pos2, y, x1, mod3, route, final_norm_g, l, bsz, seq)
    return out.reshape(bsz, seq, D_MODEL)
```

```python
import functools

import jax
import jax.numpy as jnp
from jax import lax
from jax.experimental import pallas as pl
from jax.experimental.pallas import tpu as pltpu

F32 = jnp.float32
BF16 = jnp.bfloat16

D_MODEL = 2048
S5_WIDTH = 512
S5_GROUP = 16
S5_GROUPS = 32
S5_STATE = 64
S5_NSTATE = S5_GROUPS * S5_STATE
LRU_WIDTH = 768
LRU_HEADS = 8
LRU_HEAD_DIM = 96
CONV_WIDTH = 4
LRU_C = 8.0
GLA_WIDTH = 768
GLA_HEADS = 4
GLA_DV = 192
GLA_DK = 96
GLA_GATE_RANK = 16
GLA_TAU = 16.0
GLA_CHUNK = 64
N_EXPERTS = 8
N_ADA = 6
EPS = 1e-6

LANE = 128
SUBLANE = 8
GLA_DKP = 128
GLA_DVP = 256
GLA_QW = GLA_HEADS * GLA_DKP
GLA_VW = GLA_HEADS * GLA_DVP
VMEM_LIMIT = 56 * 1024 * 1024

COL_LRU_X = 0
COL_LRU_G = 768
COL_S5 = 1536
COL_Q = 2048
COL_K = 2560
COL_V = 3072
COL_OG = 4096
COL_GLOW = 5120
NP_IN = 5248
NP_OUT_ROWS = S5_WIDTH + LRU_WIDTH + GLA_VW

TS = 512
NSEG = SUBLANE
SEG = TS // NSEG

TM_PROJ = 256
TM_ROUTE = 512
TM_COMB = 256
TM_GATHER = 512

FFN_TM = 256
FFN_GROUP = 4
FFN_FC = 256
FFN_Q_DENSE = 8
FFN_Q_EXPERT = 10


def _cparams(sem, vmem=VMEM_LIMIT):
    return pltpu.CompilerParams(dimension_semantics=sem, vmem_limit_bytes=vmem)


def _dot(a, b):
    return jnp.dot(a, b, preferred_element_type=F32)


def _resident(shape, layer=None):
    nd = len(shape)
    if layer is None:
        return pl.BlockSpec(shape, lambda *_: (0,) * nd, pipeline_mode=pl.Buffered(1))
    return pl.BlockSpec((None,) + tuple(shape), lambda *_: (layer,) + (0,) * nd,
                        pipeline_mode=pl.Buffered(1))


def _mod_kernel(c_ref, w_ref, b_ref, o_ref):
    c = c_ref[...]
    cs = (c * jax.nn.sigmoid(c)).astype(BF16)
    o_ref[...] = _dot(cs, w_ref[...].astype(BF16)) + b_ref[...]


def _adaln_mod(c, ada_w, ada_b):
    depth, d, n = ada_w.shape
    bsz = c.shape[0]
    tn = 1024
    return pl.pallas_call(
        _mod_kernel,
        grid=(depth, n // tn),
        in_specs=[
            pl.BlockSpec((bsz, d), lambda l, j: (0, 0)),
            pl.BlockSpec((None, d, tn), lambda l, j: (l, 0, j)),
            pl.BlockSpec((None, 1, tn), lambda l, j: (l, 0, j)),
        ],
        out_specs=pl.BlockSpec((None, bsz, tn), lambda l, j: (l, 0, j)),
        out_shape=jax.ShapeDtypeStruct((depth, bsz, n), F32),
        compiler_params=_cparams(("arbitrary", "arbitrary")),
        name="adaln_mod",
    )(c, ada_w, ada_b.reshape(depth, 1, n))


def _mod_spec(layer, bsz, which, tiles_per_batch):
    def imap(i, *_):
        return ((layer * bsz + i // tiles_per_batch) * N_ADA + which, 0, 0)
    return pl.BlockSpec((None, 1, D_MODEL), imap)


def _modulated_norm(x, g, scale, shift):
    ms = jnp.mean(x * x, axis=-1, keepdims=True)
    h = x * lax.rsqrt(ms + EPS) * g
    return h * (1.0 + scale) + shift


_IN_CHUNKS = tuple((s, min(512, NP_IN - s)) for s in range(0, NP_IN, 512))


def _inproj_kernel(*refs, with_residual):
    if with_residual:
        x1_ref, f_ref, gate_ref, g_ref, sh_ref, sc_ref, w_ref, o_ref, x_ref = refs
        x = x1_ref[...] + gate_ref[...] * f_ref[...]
        x_ref[...] = x
    else:
        xin_ref, g_ref, sh_ref, sc_ref, w_ref, o_ref = refs
        x = xin_ref[...]
    h = _modulated_norm(x, g_ref[...], sc_ref[...], sh_ref[...]).astype(BF16)
    for s, w in _IN_CHUNKS:
        o_ref[:, s:s + w] = _dot(h, w_ref[:, s:s + w])


def _in_projection(x2, g, mod3, w_packed, layer, bsz, seq, residual=None):
    t = x2.shape[0]
    tpb = seq // TM_PROJ
    rowblk = pl.BlockSpec((TM_PROJ, D_MODEL), lambda i: (i, 0))
    common = [_resident((1, D_MODEL), layer), _mod_spec(layer, bsz, 0, tpb), _mod_spec(layer, bsz, 1, tpb),
              _resident((D_MODEL, NP_IN), layer)]
    proj_spec = pl.BlockSpec((TM_PROJ, NP_IN), lambda i: (i, 0))
    proj_shape = jax.ShapeDtypeStruct((t, NP_IN), F32)
    if residual is None:
        in_specs = [rowblk] + common
        args = [x2, g, mod3, mod3, w_packed]
        out_specs, out_shape = proj_spec, proj_shape
    else:
        f = residual
        in_specs = [rowblk, rowblk, _mod_spec(layer - 1, bsz, 5, tpb)] + common
        args = [x2, f, mod3, g, mod3, mod3, w_packed]
        out_specs = [proj_spec, rowblk]
        out_shape = [proj_shape, jax.ShapeDtypeStruct((t, D_MODEL), F32)]
    return pl.pallas_call(
        functools.partial(_inproj_kernel, with_residual=residual is not None),
        grid=(t // TM_PROJ,),
        in_specs=in_specs, out_specs=out_specs, out_shape=out_shape,
        compiler_params=_cparams(("arbitrary",)),
        name="in_projection",
    )(*args)


S5_TS = 256
S5_NROW = SUBLANE
S5_SW = S5_NSTATE // S5_NROW
S5_LT = S5_SW // LANE
S5_PITCH = S5_TS + SUBLANE
S5_NOUT = 2
S5_RPO = S5_NROW // S5_NOUT


def _s5_kernel(u_ref, bwr_ref, bwi_ref, cw_ref, lam_ref, d_ref, wg_ref, bg_ref, ng_ref, o_ref,
               slab_ref, st_ref):
    @pl.when(pl.program_id(0) == 0)
    def _():
        st_ref[...] = jnp.zeros_like(st_ref)

    nbatch = u_ref.shape[0]
    for b in range(nbatch):
        ub = u_ref[b].astype(BF16)
        for j in range(S5_NROW):
            uj = ub[:, (j // 2) * LANE:(j // 2 + 1) * LANE]
            parts = (_dot(uj, bwr_ref[j]), _dot(uj, bwi_ref[j]))
            rows = slice(j * S5_PITCH, j * S5_PITCH + S5_TS)
            for c in range(2):
                for q in range(S5_LT):
                    slab_ref[b, c, q, rows, :] = parts[c][:, q * LANE:(q + 1) * LANE]

    lr = [lam_ref[0, :, q * LANE:(q + 1) * LANE] for q in range(S5_LT)]
    li = [lam_ref[1, :, q * LANE:(q + 1) * LANE] for q in range(S5_LT)]

    def step(k, carry):
        rows = pl.ds(k, S5_NROW, stride=S5_PITCH)
        out = []
        for b in range(nbatch):
            for q in range(S5_LT):
                hr, hi = carry[b * S5_LT + q]
                nr = lr[q] * hr - li[q] * hi + slab_ref[b, 0, q, rows, :]
                ni = lr[q] * hi + li[q] * hr + slab_ref[b, 1, q, rows, :]
                slab_ref[b, 0, q, rows, :] = nr
                slab_ref[b, 1, q, rows, :] = ni
                out.append((nr, ni))
        return tuple(out)

    init = tuple((st_ref[b, 0, q], st_ref[b, 1, q]) for b in range(nbatch) for q in range(S5_LT))
    final = lax.fori_loop(0, S5_TS, step, init, unroll=2)
    for b in range(nbatch):
        for q in range(S5_LT):
            st_ref[b, 0, q] = final[b * S5_LT + q][0]
            st_ref[b, 1, q] = final[b * S5_LT + q][1]

    for b in range(nbatch):
        ys = []
        for p in range(S5_NOUT):
            cols = []
            for c in range(2):
                for j in range(p * S5_RPO, (p + 1) * S5_RPO):
                    rows = slice(j * S5_PITCH, j * S5_PITCH + S5_TS)
                    cols += [slab_ref[b, c, q, rows, :] for q in range(S5_LT)]
            ys.append(_dot(jnp.concatenate(cols, axis=1).astype(BF16), cw_ref[p]))
        y = jnp.concatenate(ys, axis=1) + d_ref[...] * u_ref[b]
        z = jax.nn.gelu(y).astype(BF16)
        zz = _dot(z, wg_ref[...]) + bg_ref[...]
        o = zz[:, :S5_WIDTH] * jax.nn.sigmoid(zz[:, S5_WIDTH:])
        o = o * lax.rsqrt(jnp.mean(o * o, axis=-1, keepdims=True) + EPS) * ng_ref[...]
        o_ref[b] = o.astype(BF16)


def _s5_mixer(proj, p, layer, bsz, seq):
    t = bsz * seq
    ublk = COL_S5 // S5_WIDTH
    res = functools.partial(_resident, layer=layer)
    out = pl.pallas_call(
        _s5_kernel,
        grid=(seq // S5_TS,),
        in_specs=[
            pl.BlockSpec((bsz, S5_TS, S5_WIDTH), lambda s: (0, s, ublk)),
            res((S5_NROW, LANE, S5_SW)),
            res((S5_NROW, LANE, S5_SW)),
            res((S5_NOUT, 2 * S5_RPO * S5_SW, S5_WIDTH // S5_NOUT)),
            res((2, S5_NROW, S5_SW)),
            res((1, S5_WIDTH)),
            res((S5_WIDTH, 2 * S5_WIDTH)),
            res((1, 2 * S5_WIDTH)),
            res((1, S5_WIDTH)),
        ],
        out_specs=pl.BlockSpec((bsz, S5_TS, S5_WIDTH), lambda s: (0, s, 0)),
        out_shape=jax.ShapeDtypeStruct((bsz, seq, S5_WIDTH), BF16),
        scratch_shapes=[
            pltpu.VMEM((bsz, 2, S5_LT, S5_NROW * S5_PITCH, LANE), F32),
            pltpu.VMEM((bsz, 2, S5_LT, S5_NROW, LANE), F32),
        ],
        compiler_params=_cparams(("arbitrary",)),
        name="s5_mixer",
    )(proj.reshape(bsz, seq, -1), p["bw_re"], p["bw_im"], p["cw"], p["lam"],
      p["d"], p["w_glu"], p["b_glu"], p["norm_g"])
    return out.reshape(t, S5_WIDTH)


def _s5_params(a_re, a_im, log_dt, b_re, b_im, c_re, c_im, d_skip, w_glu, b_glu, norm_g):
    dt = jnp.exp(log_dt)[:, None]
    mag = jnp.exp(a_re * dt)
    abar_re = mag * jnp.cos(a_im * dt)
    abar_im = mag * jnp.sin(a_im * dt)
    nr = abar_re - 1.0
    ni = abar_im
    den = a_re * a_re + a_im * a_im
    f_re = (nr * a_re + ni * a_im) / den
    f_im = (ni * a_re - nr * a_im) / den
    bbar_re = f_re[..., None] * b_re - f_im[..., None] * b_im
    bbar_im = f_re[..., None] * b_im + f_im[..., None] * b_re
    eye = jnp.eye(S5_GROUPS, dtype=F32)
    bbd = lambda m: (jnp.transpose(m, (0, 2, 1))[:, :, None, :] * eye[:, None, :, None]
                     ).reshape(S5_WIDTH, S5_NSTATE)
    cbd = lambda m: (jnp.transpose(m, (0, 2, 1))[:, :, None, :] * eye[:, None, :, None]
                     ).reshape(S5_NSTATE, S5_WIDTH)
    bw = lambda m: jnp.stack([m[(j // 2) * LANE:(j // 2 + 1) * LANE, j * S5_SW:(j + 1) * S5_SW]
                              for j in range(S5_NROW)]).astype(BF16)
    cre, cim = cbd(c_re), cbd(c_im)
    ns, nc = S5_RPO * S5_SW, S5_WIDTH // S5_NOUT
    cw = jnp.stack([jnp.concatenate([cre[p * ns:(p + 1) * ns, p * nc:(p + 1) * nc],
                                     -cim[p * ns:(p + 1) * ns, p * nc:(p + 1) * nc]], axis=0)
                    for p in range(S5_NOUT)]).astype(BF16)
    lam = jnp.stack([abar_re.reshape(S5_NROW, S5_SW), abar_im.reshape(S5_NROW, S5_SW)])
    return {
        "bw_re": bw(bbd(bbar_re)), "bw_im": bw(bbd(bbar_im)), "cw": cw, "lam": lam,
        "d": d_skip.reshape(1, S5_WIDTH), "w_glu": w_glu.astype(BF16),
        "b_glu": b_glu.reshape(1, -1), "norm_g": norm_g.reshape(1, S5_WIDTH),
    }


LRU_LT = LRU_WIDTH // LANE
LRU_PITCH = SEG + SUBLANE


def _lru_kernel(x_ref, gb_ref, cw_ref, cb_ref, wa_ref, ba_ref, wx_ref, bx_ref, lam_ref, ng_ref,
                o_ref, xe_ref, a_ref, b_ref, hc_ref):
    first = pl.program_id(1) == 0

    @pl.when(first)
    def _():
        xe_ref[0:SUBLANE, :] = jnp.zeros((SUBLANE, LRU_WIDTH), F32)
        hc_ref[...] = jnp.zeros_like(hc_ref)

    @pl.when(jnp.logical_not(first))
    def _():
        xe_ref[0:SUBLANE, :] = xe_ref[TS:TS + SUBLANE, :]

    xe_ref[SUBLANE:TS + SUBLANE, :] = x_ref[...]
    xc = cb_ref[...]
    for w in range(CONV_WIDTH):
        off = SUBLANE - (CONV_WIDTH - 1) + w
        xc = xc + cw_ref[w:w + 1, :] * xe_ref[off:off + TS, :]
    xb = xc.astype(BF16)
    r = jax.nn.sigmoid(_dot(xb, wa_ref[...]) + ba_ref[...])
    gi = jax.nn.sigmoid(_dot(xb, wx_ref[...]) + bx_ref[...])
    log_a = LRU_C * r * jax.nn.log_sigmoid(lam_ref[...])
    a = jnp.exp(log_a)
    one_minus_a2 = -jnp.tanh(log_a) * (1.0 + a * a)
    bt = jnp.sqrt(jnp.maximum(one_minus_a2, 0.0)) * gi * xc
    for j in range(NSEG):
        src = slice(j * SEG, (j + 1) * SEG)
        dst = slice(j * LRU_PITCH, j * LRU_PITCH + SEG)
        for q in range(LRU_LT):
            a_ref[q, dst, :] = a[src, q * LANE:(q + 1) * LANE]
            b_ref[q, dst, :] = bt[src, q * LANE:(q + 1) * LANE]

    def local_scan(k, carry):
        rows = pl.ds(k, NSEG, stride=LRU_PITCH)
        out = []
        for q in range(LRU_LT):
            h, ap = carry[q]
            aq = a_ref[q, rows, :]
            h = aq * h + b_ref[q, rows, :]
            ap = ap * aq
            b_ref[q, rows, :] = h
            a_ref[q, rows, :] = ap
            out.append((h, ap))
        return tuple(out)

    init = (jnp.zeros((NSEG, LANE), F32), jnp.ones((NSEG, LANE), F32))
    ends = lax.fori_loop(0, SEG, local_scan, tuple(init for _ in range(LRU_LT)))
    e = jnp.concatenate([ends[q][0] for q in range(LRU_LT)], axis=1)
    ap = jnp.concatenate([ends[q][1] for q in range(LRU_LT)], axis=1)

    s = hc_ref[...]
    g = gb_ref[...]
    for j in range(NSEG):
        rs = slice(j * SEG, (j + 1) * SEG)
        ps = slice(j * LRU_PITCH, j * LRU_PITCH + SEG)
        hl = jnp.concatenate([b_ref[q, ps, :] for q in range(LRU_LT)], axis=1)
        al = jnp.concatenate([a_ref[q, ps, :] for q in range(LRU_LT)], axis=1)
        h = hl + al * s
        o = h * jax.nn.gelu(g[rs, :])
        o = o * lax.rsqrt(jnp.mean(o * o, axis=-1, keepdims=True) + EPS) * ng_ref[...]
        o_ref[rs, :] = o.astype(BF16)
        s = e[j:j + 1] + ap[j:j + 1] * s
    hc_ref[...] = s


def _lru_mixer(proj, p, layer, bsz, seq):
    t = bsz * seq
    nb = seq // TS
    res = functools.partial(_resident, layer=layer)
    vec = res((1, LRU_WIDTH))
    return pl.pallas_call(
        _lru_kernel,
        grid=(bsz, nb),
        in_specs=[
            pl.BlockSpec((TS, LRU_WIDTH), lambda b, s: (b * nb + s, COL_LRU_X // LRU_WIDTH)),
            pl.BlockSpec((TS, LRU_WIDTH), lambda b, s: (b * nb + s, COL_LRU_G // LRU_WIDTH)),
            res((CONV_WIDTH, LRU_WIDTH)), vec,
            res((LRU_WIDTH, LRU_WIDTH)), vec,
            res((LRU_WIDTH, LRU_WIDTH)), vec,
            vec, vec,
        ],
        out_specs=pl.BlockSpec((TS, LRU_WIDTH), lambda b, s: (b * nb + s, 0)),
        out_shape=jax.ShapeDtypeStruct((t, LRU_WIDTH), BF16),
        scratch_shapes=[
            pltpu.VMEM((TS + SUBLANE, LRU_WIDTH), F32),
            pltpu.VMEM((LRU_LT, NSEG * LRU_PITCH, LANE), F32),
            pltpu.VMEM((LRU_LT, NSEG * LRU_PITCH, LANE), F32),
            pltpu.VMEM((1, LRU_WIDTH), F32),
        ],
        compiler_params=_cparams(("arbitrary", "arbitrary")),
        name="rglru_mixer",
    )(proj, proj, p["conv_w"], p["conv_b"], p["wa"], p["ba"], p["wx"], p["bx"], p["lam"], p["norm_g"])


def _lru_params(conv_w, conv_b, w_a, b_a, w_x, b_x, lam, norm_g):
    eye = jnp.eye(LRU_HEADS, dtype=F32)
    bd = lambda m: (m[:, :, None, :] * eye[:, None, :, None]).reshape(LRU_WIDTH, LRU_WIDTH).astype(BF16)
    row = lambda v: v.reshape(1, LRU_WIDTH)
    return {"conv_w": conv_w, "conv_b": row(conv_b), "wa": bd(w_a), "ba": row(b_a),
            "wx": bd(w_x), "bx": row(b_x), "lam": row(lam), "norm_g": row(norm_g)}


def _split3(x):
    hi = x.astype(BF16)
    r1 = x - hi.astype(F32)
    mid = r1.astype(BF16)
    lo = (r1 - mid.astype(F32)).astype(BF16)
    return hi, mid, lo


def _gla_kernel(q_ref, k_ref, v_ref, og_ref, gl_ref, wg_ref, bg_ref, hg_ref, o_ref, st_ref):
    @pl.when(pl.program_id(1) == 0)
    def _():
        st_ref[...] = jnp.zeros_like(st_ref)

    c = GLA_CHUNK
    hc = GLA_HEADS * c
    ri = lax.broadcasted_iota(jnp.int32, (c, c), 0)
    ci = lax.broadcasted_iota(jnp.int32, (c, c), 1)
    tri = (ri >= ci).astype(BF16)
    rr = lax.broadcasted_iota(jnp.int32, (hc, hc), 0)
    cc = lax.broadcasted_iota(jnp.int32, (hc, hc), 1)
    causal = jnp.logical_and(rr // c == cc // c, rr % c >= cc % c)
    heads = lambda m, w: jnp.concatenate([m[:, h * w:(h + 1) * w] for h in range(GLA_HEADS)], axis=0)

    for n in range(TS // c):
        rs = slice(n * c, (n + 1) * c)
        logit = _dot(gl_ref[rs, :].astype(BF16), wg_ref[...]) + bg_ref[...]
        la = jax.nn.log_sigmoid(logit) / GLA_TAU
        hi, mid, lo = _split3(la)
        bcum = _dot(tri, hi) + _dot(tri, mid) + _dot(tri, lo)
        b_last = bcum[c - 1:c, :]
        kk = k_ref[rs, :]
        q_dec = q_ref[rs, :] * (GLA_DK ** -0.5) * jnp.exp(bcum)
        k_in = kk * jnp.exp(-bcum)
        k_out = (kk * jnp.exp(b_last - bcum)).astype(BF16)
        gamma = jnp.exp(b_last)
        vb = v_ref[rs, :].astype(BF16)
        qd = q_dec.astype(BF16)

        att = lax.dot_general(heads(qd, GLA_DKP), heads(k_in.astype(BF16), GLA_DKP),
                              (((1,), (1,)), ((), ())), preferred_element_type=F32)
        att = jnp.where(causal, att, 0.0).astype(BF16)
        o_s = _dot(att, heads(vb, GLA_DVP))
        outs = []
        for h in range(GLA_HEADS):
            ks = slice(h * GLA_DKP, (h + 1) * GLA_DKP)
            vs = slice(h * GLA_DVP, (h + 1) * GLA_DVP)
            st = st_ref[h]
            o_h = o_s[h * c:(h + 1) * c, :] + lax.dot_general(
                qd[:, ks], st.astype(BF16), (((1,), (1,)), ((), ())), preferred_element_type=F32)
            delta = lax.dot_general(vb[:, vs], k_out[:, ks], (((0,), (0,)), ((), ())),
                                    preferred_element_type=F32)
            st_ref[h] = gamma[:, ks] * st + delta
            ms = jnp.sum(o_h * o_h, axis=-1, keepdims=True) * (1.0 / GLA_DV)
            outs.append(o_h * lax.rsqrt(ms + EPS) * hg_ref[...])
        o = jnp.concatenate(outs, axis=1)
        og = og_ref[rs, :]
        o_ref[rs, :] = (o * (og * jax.nn.sigmoid(og))).astype(BF16)


def _gla_mixer(proj, p, layer, bsz, seq):
    t = bsz * seq
    nb = seq // TS
    row = lambda b, s: b * nb + s
    res = functools.partial(_resident, layer=layer)
    return pl.pallas_call(
        _gla_kernel,
        grid=(bsz, nb),
        in_specs=[
            pl.BlockSpec((TS, GLA_QW), lambda b, s: (row(b, s), COL_Q // GLA_QW)),
            pl.BlockSpec((TS, GLA_QW), lambda b, s: (row(b, s), COL_K // GLA_QW)),
            pl.BlockSpec((TS, GLA_VW), lambda b, s: (row(b, s), COL_V // GLA_VW)),
            pl.BlockSpec((TS, GLA_VW), lambda b, s: (row(b, s), COL_OG // GLA_VW)),
            pl.BlockSpec((TS, LANE), lambda b, s: (row(b, s), COL_GLOW // LANE)),
            res((LANE, GLA_QW)),
            res((1, GLA_QW)),
            res((1, GLA_DVP)),
        ],
        out_specs=pl.BlockSpec((TS, GLA_VW), lambda b, s: (row(b, s), 0)),
        out_shape=jax.ShapeDtypeStruct((t, GLA_VW), BF16),
        scratch_shapes=[pltpu.VMEM((GLA_HEADS, GLA_DVP, GLA_DKP), F32)],
        compiler_params=_cparams(("arbitrary", "arbitrary")),
        name="gla_mixer",
    )(proj, proj, proj, proj, proj, p["w_g2"], p["b_g2"], p["head_g"])


def _pad_heads(m, d, dp):
    lead = m.shape[:-1]
    m = m.reshape(lead + (GLA_HEADS, d))
    m = jnp.pad(m, [(0, 0)] * len(lead) + [(0, 0), (0, dp - d)])
    return m.reshape(lead + (GLA_HEADS * dp,))


def _gla_params(w_g2, b_g2, head_g):
    w = jnp.pad(_pad_heads(w_g2, GLA_DK, GLA_DKP), ((0, LANE - GLA_GATE_RANK), (0, 0)))
    return {"w_g2": w.astype(BF16),
            "b_g2": _pad_heads(b_g2, GLA_DK, GLA_DKP).reshape(1, GLA_QW),
            "head_g": jnp.pad(head_g, (0, GLA_DVP - GLA_DV)).reshape(1, GLA_DVP)}


def _pack_w_in(w):
    s5 = w[:, 0:512]
    lx = w[:, 512:1280]
    lg = w[:, 1280:2048]
    q = w[:, 2048:2432]
    k = w[:, 2432:2816]
    v = w[:, 2816:3584]
    gl = w[:, 3584:3600]
    og = w[:, 3600:4368]
    cols = [lx, lg, s5, _pad_heads(q, GLA_DK, GLA_DKP), _pad_heads(k, GLA_DK, GLA_DKP),
            _pad_heads(v, GLA_DV, GLA_DVP), _pad_heads(og, GLA_DV, GLA_DVP),
            jnp.pad(gl, ((0, 0), (0, LANE - GLA_GATE_RANK)))]
    return jnp.concatenate(cols, axis=1).astype(BF16)


def _pack_w_out(w):
    gla = w[S5_WIDTH + LRU_WIDTH:].reshape(GLA_HEADS, GLA_DV, D_MODEL)
    gla = jnp.pad(gla, ((0, 0), (0, GLA_DVP - GLA_DV), (0, 0))).reshape(GLA_VW, D_MODEL)
    return jnp.concatenate([w[:S5_WIDTH + LRU_WIDTH], gla], axis=0).astype(BF16)


def _outproj_kernel(ys_ref, yl_ref, yg_ref, x_ref, w_ref, gate_ref, g_ref, sh_ref, sc_ref, *rest,
                    with_router):
    r0, r1 = S5_WIDTH, S5_WIDTH + LRU_WIDTH
    mix = (_dot(ys_ref[...], w_ref[0:r0, :]) + _dot(yl_ref[...], w_ref[r0:r1, :])
           + _dot(yg_ref[...], w_ref[r1:NP_OUT_ROWS, :]))
    x1 = x_ref[...] + gate_ref[...] * mix
    h = _modulated_norm(x1, g_ref[...], sc_ref[...], sh_ref[...])
    if with_router:
        wr_hi_ref, wr_lo_ref, x1_ref, h_ref, lg_ref = rest
        h_ref[...] = h
        hh = h.astype(BF16)
        hm = (h - hh.astype(F32)).astype(BF16)
        lg_ref[...] = (_dot(hh, wr_hi_ref[...]) + _dot(hm, wr_hi_ref[...])
                       + _dot(hh, wr_lo_ref[...]))
    else:
        x1_ref, h_ref = rest
        h_ref[...] = h.astype(BF16)
    x1_ref[...] = x1


def _out_projection(ys, yl, yg, x2, w_packed, g, mod3, layer, bsz, seq, router=None):
    t = x2.shape[0]
    tpb = seq // TM_PROJ
    rowblk = lambda w: pl.BlockSpec((TM_PROJ, w), lambda i: (i, 0))
    in_specs = [rowblk(S5_WIDTH), rowblk(LRU_WIDTH), rowblk(GLA_VW), rowblk(D_MODEL),
                _resident((NP_OUT_ROWS, D_MODEL), layer),
                _mod_spec(layer, bsz, 2, tpb),
                _resident((1, D_MODEL), layer),
                _mod_spec(layer, bsz, 3, tpb), _mod_spec(layer, bsz, 4, tpb)]
    args = [ys, yl, yg, x2, w_packed, mod3, g, mod3, mod3]
    if router is None:
        out_specs = [rowblk(D_MODEL), rowblk(D_MODEL)]
        out_shape = [jax.ShapeDtypeStruct((t, D_MODEL), F32), jax.ShapeDtypeStruct((t, D_MODEL), BF16)]
    else:
        wr = jnp.pad(router, ((0, 0), (0, LANE - N_EXPERTS)))
        wr_hi = wr.astype(BF16)
        wr_lo = (wr - wr_hi.astype(F32)).astype(BF16)
        in_specs += [_resident((D_MODEL, LANE)), _resident((D_MODEL, LANE))]
        args += [wr_hi, wr_lo]
        out_specs = [rowblk(D_MODEL), rowblk(D_MODEL), rowblk(LANE)]
        out_shape = [jax.ShapeDtypeStruct((t, D_MODEL), F32), jax.ShapeDtypeStruct((t, D_MODEL), F32),
                     jax.ShapeDtypeStruct((t, LANE), F32)]
    return pl.pallas_call(
        functools.partial(_outproj_kernel, with_router=router is not None),
        grid=(t // TM_PROJ,),
        in_specs=in_specs, out_specs=out_specs, out_shape=out_shape,
        compiler_params=_cparams(("arbitrary",)),
        name="out_projection",
    )(*args)


def _ffn_kernel(se_ref, sn_ref, sx_ref, x_ref, w1_ref, w3_ref, w2_ref, o_ref, w1b_ref, w3b_ref, w2b_ref,
                *, q_tiles):
    s = pl.program_id(0)
    n = sn_ref[s]

    @pl.when(pl.program_id(1) == 0)
    def _():
        o_ref[...] = jnp.zeros_like(o_ref)

    @pl.when(n > 0)
    def _():
        w1b_ref[...] = w1_ref[...].astype(BF16)
        w3b_ref[...] = w3_ref[...].astype(BF16)
        w2b_ref[...] = w2_ref[...].astype(BF16)

    def swiglu_rows(first, count):
        rows = slice(first * FFN_TM, (first + count) * FFN_TM)
        xq = x_ref[rows, :]
        h1 = _dot(xq, w1b_ref[...])
        h3 = _dot(xq, w3b_ref[...])
        a = (h1 * jax.nn.sigmoid(h1) * h3).astype(BF16)
        o_ref[rows, :] += _dot(a, w2b_ref[...])

    for g0 in range(0, q_tiles, FFN_GROUP):
        gsz = min(FFN_GROUP, q_tiles - g0)
        if gsz > 1:
            pl.when(n >= g0 + gsz)(functools.partial(swiglu_rows, g0, gsz))
        for q in range(g0, g0 + gsz):
            cond = jnp.logical_and(q < n, n < g0 + gsz) if gsz > 1 else q < n
            pl.when(cond)(functools.partial(swiglu_rows, q, 1))


def _ffn(tables, xs, w1, w3, w2, q_tiles):
    r, d = xs.shape
    dff = w1.shape[2]
    nf = dff // FFN_FC
    rows = q_tiles * FFN_TM
    ns = r // rows

    def fidx(f, s, sn):
        return jnp.where(sn[s] > 0, f, nf - 1)

    grid_spec = pltpu.PrefetchScalarGridSpec(
        num_scalar_prefetch=3,
        grid=(ns, nf),
        in_specs=[
            pl.BlockSpec((rows, d), lambda s, f, se, sn, sx: (sx[s], 0), pipeline_mode=pl.Buffered(1)),
            pl.BlockSpec((None, d, FFN_FC), lambda s, f, se, sn, sx: (se[s], 0, fidx(f, s, sn))),
            pl.BlockSpec((None, d, FFN_FC), lambda s, f, se, sn, sx: (se[s], 0, fidx(f, s, sn))),
            pl.BlockSpec((None, FFN_FC, d), lambda s, f, se, sn, sx: (se[s], fidx(f, s, sn), 0)),
        ],
        out_specs=pl.BlockSpec((rows, d), lambda s, f, se, sn, sx: (s, 0),
                               pipeline_mode=pl.Buffered(1)),
        scratch_shapes=[pltpu.VMEM((d, FFN_FC), BF16), pltpu.VMEM((d, FFN_FC), BF16),
                        pltpu.VMEM((FFN_FC, d), BF16)],
    )
    return pl.pallas_call(
        functools.partial(_ffn_kernel, q_tiles=q_tiles),
        grid_spec=grid_spec,
        out_shape=jax.ShapeDtypeStruct((r, d), F32),
        compiler_params=_cparams(("arbitrary", "arbitrary")),
        name="ffn_swiglu",
    )(*tables, xs, w1, w3, w2)


def _dense_tables(t):
    ns = t // (FFN_Q_DENSE * FFN_TM)
    return (jnp.zeros((ns,), jnp.int32), jnp.full((ns,), FFN_Q_DENSE, jnp.int32),
            jnp.arange(ns, dtype=jnp.int32))


def _route_kernel(lg_ref, o_ref, cnt_ref, carry_ref):
    @pl.when(pl.program_id(0) == 0)
    def _():
        carry_ref[...] = jnp.zeros_like(carry_ref)

    tm = TM_ROUTE
    lane = lax.broadcasted_iota(jnp.int32, (tm, LANE), 1)
    neg = jnp.float32(-jnp.inf)
    l = jnp.where(lane < N_EXPERTS, lg_ref[...], neg)
    m1 = jnp.max(l, axis=1, keepdims=True)
    i1 = jnp.min(jnp.where(l == m1, lane, LANE), axis=1, keepdims=True)
    oh1 = lane == i1
    l2 = jnp.where(oh1, neg, l)
    m2 = jnp.max(l2, axis=1, keepdims=True)
    i2 = jnp.min(jnp.where(l2 == m2, lane, LANE), axis=1, keepdims=True)
    oh2 = lane == i2
    e = jnp.exp(m2 - m1)
    den = 1.0 + e
    g1 = 1.0 / den
    g2 = e / den
    oh = jnp.where(jnp.logical_or(oh1, oh2), 1.0, 0.0)
    ri = lax.broadcasted_iota(jnp.int32, (tm, tm), 0)
    ci = lax.broadcasted_iota(jnp.int32, (tm, tm), 1)
    before = (ri > ci).astype(BF16)
    cum = _dot(before, oh.astype(BF16)) + carry_ref[...]
    r1 = jnp.sum(jnp.where(oh1, cum, 0.0), axis=1, keepdims=True)
    r2 = jnp.sum(jnp.where(oh2, cum, 0.0), axis=1, keepdims=True)
    carry_ref[...] = carry_ref[...] + jnp.sum(oh, axis=0, keepdims=True)
    cnt_ref[...] = carry_ref[...]
    out = jnp.where(lane == 0, i1.astype(F32), 0.0)
    out = jnp.where(lane == 1, i2.astype(F32), out)
    out = jnp.where(lane == 2, g1, out)
    out = jnp.where(lane == 3, g2, out)
    out = jnp.where(lane == 4, r1, out)
    out = jnp.where(lane == 5, r2, out)
    o_ref[...] = out


def _route(logits):
    t = logits.shape[0]
    return pl.pallas_call(
        _route_kernel,
        grid=(t // TM_ROUTE,),
        in_specs=[pl.BlockSpec((TM_ROUTE, LANE), lambda i: (i, 0))],
        out_specs=[pl.BlockSpec((TM_ROUTE, LANE), lambda i: (i, 0)),
                   pl.BlockSpec((1, LANE), lambda i: (0, 0))],
        out_shape=[jax.ShapeDtypeStruct((t, LANE), F32), jax.ShapeDtypeStruct((1, LANE), F32)],
        scratch_shapes=[pltpu.VMEM((1, LANE), F32)],
        compiler_params=_cparams(("arbitrary",)),
        name="route_top2",
    )(logits)


def _expert_tables(route, counts, t):
    qt = FFN_Q_EXPERT
    rows = qt * FFN_TM
    max_tiles = (2 * t + FFN_TM - 1) // FFN_TM + N_EXPERTS
    ns = (max_tiles + N_EXPERTS * (qt - 1)) // qt
    i1 = route[:, 0].astype(jnp.int32)
    i2 = route[:, 1].astype(jnp.int32)
    r1 = route[:, 4].astype(jnp.int32)
    r2 = route[:, 5].astype(jnp.int32)
    cnt = counts[0, :N_EXPERTS].astype(jnp.int32)
    tiles = (cnt + FFN_TM - 1) // FFN_TM
    supers = (tiles + qt - 1) // qt
    super_end = jnp.cumsum(supers)
    super_start = super_end - supers
    n_used = super_end[-1]
    ids = jnp.arange(ns, dtype=jnp.int32)
    sx = jnp.minimum(ids, n_used - 1)
    se = jnp.minimum(jnp.sum((sx[:, None] >= super_end[None, :]).astype(jnp.int32), axis=1),
                     N_EXPERTS - 1)
    sn = jnp.clip(tiles[se] - (sx - super_start[se]) * qt, 0, qt)
    sn = jnp.where(ids < n_used, sn, 0).astype(jnp.int32)
    row_off = super_start * rows
    pos1 = row_off[i1] + r1
    pos2 = row_off[i2] + r2
    tok = jnp.arange(t, dtype=jnp.int32)
    src = jnp.zeros((ns * rows,), jnp.int32).at[pos1].set(tok).at[pos2].set(tok)
    gpt = rows // TM_GATHER
    gid = jnp.arange(ns * gpt, dtype=jnp.int32)
    gv = ((gid % gpt) * TM_GATHER < sn[gid // gpt] * FFN_TM).astype(jnp.int32)
    return (se, sn, sx), pos1, pos2, src, gv


def _row_copy(src_hbm, row, dst, slot, sem):
    return pltpu.make_async_copy(src_hbm.at[pl.ds(row, 1), :], dst.at[pl.ds(slot, 1), :], sem)


def _tile_copy(src_hbm, dst, sem):
    return pltpu.make_async_copy(src_hbm.at[pl.ds(0, dst.shape[0]), :], dst, sem)


def _gather_kernel(src_ref, gv_ref, h_hbm, o_ref, buf_ref, sem):
    i = pl.program_id(0)
    slot = i % 2

    def issue(tile, sl):
        base = tile * TM_GATHER

        def body(r, c):
            _row_copy(h_hbm, src_ref[base + r], buf_ref.at[sl], r, sem.at[sl]).start()
            return c

        lax.fori_loop(0, TM_GATHER, body, 0, unroll=8)

    @pl.when(jnp.logical_and(i == 0, gv_ref[0] == 1))
    def _():
        issue(0, 0)

    nxt = jnp.minimum(i + 1, pl.num_programs(0) - 1)

    @pl.when(jnp.logical_and(i + 1 < pl.num_programs(0), gv_ref[nxt] == 1))
    def _():
        issue(i + 1, 1 - slot)

    @pl.when(gv_ref[i] == 1)
    def _():
        _tile_copy(h_hbm, buf_ref.at[slot], sem.at[slot]).wait()
        o_ref[...] = buf_ref[slot].astype(BF16)

    @pl.when(gv_ref[i] == 0)
    def _():
        o_ref[...] = jnp.zeros_like(o_ref)


def _gather_rows(src, gv, h):
    nt = gv.shape[0]
    d = h.shape[1]
    grid_spec = pltpu.PrefetchScalarGridSpec(
        num_scalar_prefetch=2,
        grid=(nt,),
        in_specs=[pl.BlockSpec(memory_space=pl.ANY)],
        out_specs=pl.BlockSpec((TM_GATHER, d), lambda i, src, gv: (i, 0)),
        scratch_shapes=[pltpu.VMEM((2, TM_GATHER, d), F32), pltpu.SemaphoreType.DMA((2,))],
    )
    return pl.pallas_call(
        _gather_kernel,
        grid_spec=grid_spec,
        out_shape=jax.ShapeDtypeStruct((nt * TM_GATHER, d), BF16),
        compiler_params=_cparams(("arbitrary",)),
        name="expert_gather",
    )(src, gv, h)


def _combine_kernel(p1_ref, p2_ref, y_hbm, x1_ref, gate_ref, rt_ref, fg_ref, o_ref, ya_ref, yb_ref, sem):
    base = pl.program_id(0) * TM_COMB

    def issue(r, c):
        _row_copy(y_hbm, p1_ref[base + r], ya_ref, r, sem.at[0]).start()
        _row_copy(y_hbm, p2_ref[base + r], yb_ref, r, sem.at[1]).start()
        return c

    lax.fori_loop(0, TM_COMB, issue, 0, unroll=8)
    _tile_copy(y_hbm, ya_ref, sem.at[0]).wait()
    _tile_copy(y_hbm, yb_ref, sem.at[1]).wait()
    rt = rt_ref[...]
    f = rt[:, 2:3] * ya_ref[...] + rt[:, 3:4] * yb_ref[...]
    x2 = x1_ref[...] + gate_ref[...] * f
    o_ref[...] = x2 * lax.rsqrt(jnp.mean(x2 * x2, axis=-1, keepdims=True) + EPS) * fg_ref[...]


def _combine(pos1, pos2, y, x1, mod3, route, final_g, layer, bsz, seq):
    t, d = x1.shape
    tpb = seq // TM_COMB
    grid_spec = pltpu.PrefetchScalarGridSpec(
        num_scalar_prefetch=2,
        grid=(t // TM_COMB,),
        in_specs=[
            pl.BlockSpec(memory_space=pl.ANY),
            pl.BlockSpec((TM_COMB, d), lambda i, p1, p2: (i, 0)),
            pl.BlockSpec((None, 1, d), lambda i, p1, p2: ((layer * bsz + i // tpb) * N_ADA + 5, 0, 0)),
            pl.BlockSpec((TM_COMB, LANE), lambda i, p1, p2: (i, 0)),
            pl.BlockSpec((1, d), lambda i, p1, p2: (0, 0)),
        ],
        out_specs=pl.BlockSpec((TM_COMB, d), lambda i, p1, p2: (i, 0)),
        scratch_shapes=[pltpu.VMEM((TM_COMB, d), F32), pltpu.VMEM((TM_COMB, d), F32),
                        pltpu.SemaphoreType.DMA((2,))],
    )
    return pl.pallas_call(
        _combine_kernel,
        grid_spec=grid_spec,
        out_shape=jax.ShapeDtypeStruct((t, d), F32),
        compiler_params=_cparams(("arbitrary",)),
        name="expert_combine_norm",
    )(pos1, pos2, y, x1, mod3, route, final_g.reshape(1, d))


def kernel(x, c, ada_w, ada_b, norm_mix_g, norm_ffn_g, w_in, s5_a_re, s5_a_im, s5_log_dt, s5_b_re, s5_b_im, s5_c_re, s5_c_im, s5_d, s5_w_glu, s5_b_glu, s5_norm_g, lru_conv_w, lru_conv_b, lru_w_a, lru_b_a, lru_w_x, lru_b_x, lru_lambda, lru_norm_g, gla_w_g2, gla_b_g2, gla_head_g, w_out, ffn_w1, ffn_w3, ffn_w2, moe_router, moe_w1, moe_w3, moe_w2, final_norm_g):
    bsz, seq, d = x.shape
    depth = ada_w.shape[0]
    t = bsz * seq
    assert depth == 2 and d == D_MODEL and seq % TS == 0 and t % (FFN_Q_DENSE * FFN_TM) == 0
    mod3 = _adaln_mod(c, ada_w, ada_b).reshape(depth * bsz * N_ADA, 1, D_MODEL)
    s5p = jax.vmap(_s5_params)(s5_a_re, s5_a_im, s5_log_dt, s5_b_re, s5_b_im, s5_c_re, s5_c_im, s5_d,
                               s5_w_glu, s5_b_glu, s5_norm_g)
    lrup = jax.vmap(_lru_params)(lru_conv_w, lru_conv_b, lru_w_a, lru_b_a, lru_w_x, lru_b_x,
                                 lru_lambda, lru_norm_g)
    glap = jax.vmap(_gla_params)(gla_w_g2, gla_b_g2, gla_head_g)
    w_in_p = jax.vmap(_pack_w_in)(w_in)
    w_out_p = jax.vmap(_pack_w_out)(w_out)
    g_mix = norm_mix_g.reshape(depth, 1, D_MODEL)
    g_ffn = norm_ffn_g.reshape(depth, 1, D_MODEL)

    xf = x.reshape(t, D_MODEL)
    pending = None
    out = None
    for l in range(depth):
        if pending is None:
            proj = _in_projection(xf, g_mix, mod3, w_in_p, l, bsz, seq)
        else:
            proj, xf = _in_projection(xf, g_mix, mod3, w_in_p, l, bsz, seq, residual=pending)
        y_s5 = _s5_mixer(proj, s5p, l, bsz, seq)
        y_lru = _lru_mixer(proj, lrup, l, bsz, seq)
        y_gla = _gla_mixer(proj, glap, l, bsz, seq)
        if l % 2 == 0:
            xf, h = _out_projection(y_s5, y_lru, y_gla, xf, w_out_p, g_ffn, mod3, l, bsz, seq)
            k = l // 2
            pending = _ffn(_dense_tables(t), h, ffn_w1[k:k + 1], ffn_w3[k:k + 1], ffn_w2[k:k + 1],
                           FFN_Q_DENSE)
        else:
            x1, h, logits = _out_projection(y_s5, y_lru, y_gla, xf, w_out_p, g_ffn, mod3, l, bsz, seq,
                                            router=moe_router[l // 2])
            route, counts = _route(logits)
            tables, pos1, pos2, src, gv = _expert_tables(route, counts, t)
            xs = _gather_rows(src, gv, h)
            y = _ffn(tables, xs, moe_w1[l // 2], moe_w3[l // 2], moe_w2[l // 2], FFN_Q_EXPERT)
            out = _combine(pos1, pos2, y, x1, mod3, route, final_norm_g, l, bsz, seq)
    return out.reshape(bsz, seq, D_MODEL)
```

```python
import functools

import jax
import jax.numpy as jnp
from jax import lax
from jax.experimental import pallas as pl
from jax.experimental.pallas import tpu as pltpu

F32 = jnp.float32
BF16 = jnp.bfloat16

D_MODEL = 2048
S5_WIDTH = 512
S5_GROUP = 16
S5_GROUPS = 32
S5_STATE = 64
S5_NSTATE = S5_GROUPS * S5_STATE
LRU_WIDTH = 768
LRU_HEADS = 8
LRU_HEAD_DIM = 96
CONV_WIDTH = 4
LRU_C = 8.0
GLA_WIDTH = 768
GLA_HEADS = 4
GLA_DV = 192
GLA_DK = 96
GLA_GATE_RANK = 16
GLA_TAU = 16.0
GLA_CHUNK = 64
N_EXPERTS = 8
N_ADA = 6
EPS = 1e-6

LANE = 128
SUBLANE = 8
GLA_DKP = 128
GLA_DVP = 256
GLA_QW = GLA_HEADS * GLA_DKP
GLA_VW = GLA_HEADS * GLA_DVP
VMEM_LIMIT = 56 * 1024 * 1024

COL_LRU_X = 0
COL_LRU_G = 768
COL_S5 = 1536
COL_Q = 2048
COL_K = 2560
COL_V = 3072
COL_OG = 4096
COL_GLOW = 5120
NP_IN = 5248
NP_OUT_ROWS = S5_WIDTH + LRU_WIDTH + GLA_VW

TS = 512
NSEG = SUBLANE
SEG = TS // NSEG

TM_PROJ = 256
TM_ROUTE = 512
TM_COMB = 256
TM_GATHER = 512

FFN_TM = 256
FFN_GROUP = 4
FFN_FC = 256
FFN_Q_DENSE = 8
FFN_Q_EXPERT = 10


def _cparams(sem, vmem=VMEM_LIMIT):
    return pltpu.CompilerParams(dimension_semantics=sem, vmem_limit_bytes=vmem)


def _dot(a, b):
    return jnp.dot(a, b, preferred_element_type=F32)


def _resident(shape, layer=None):
    nd = len(shape)
    if layer is None:
        return pl.BlockSpec(shape, lambda *_: (0,) * nd, pipeline_mode=pl.Buffered(1))
    return pl.BlockSpec((None,) + tuple(shape), lambda *_: (layer,) + (0,) * nd,
                        pipeline_mode=pl.Buffered(1))


def _mod_kernel(c_ref, w_ref, b_ref, o_ref):
    c = c_ref[...]
    cs = (c * jax.nn.sigmoid(c)).astype(BF16)
    o_ref[...] = _dot(cs, w_ref[...].astype(BF16)) + b_ref[...]


def _adaln_mod(c, ada_w, ada_b):
    depth, d, n = ada_w.shape
    bsz = c.shape[0]
    tn = 1024
    return pl.pallas_call(
        _mod_kernel,
        grid=(depth, n // tn),
        in_specs=[
            pl.BlockSpec((bsz, d), lambda l, j: (0, 0)),
            pl.BlockSpec((None, d, tn), lambda l, j: (l, 0, j)),
            pl.BlockSpec((None, 1, tn), lambda l, j: (l, 0, j)),
        ],
        out_specs=pl.BlockSpec((None, bsz, tn), lambda l, j: (l, 0, j)),
        out_shape=jax.ShapeDtypeStruct((depth, bsz, n), F32),
        compiler_params=_cparams(("arbitrary", "arbitrary")),
        name="adaln_mod",
    )(c, ada_w, ada_b.reshape(depth, 1, n))


def _mod_spec(layer, bsz, which, tiles_per_batch):
    def imap(i, *_):
        return ((layer * bsz + i // tiles_per_batch) * N_ADA + which, 0, 0)
    return pl.BlockSpec((None, 1, D_MODEL), imap)


def _modulated_norm(x, g, scale, shift):
    ms = jnp.mean(x * x, axis=-1, keepdims=True)
    h = x * lax.rsqrt(ms + EPS) * g
    return h * (1.0 + scale) + shift


_IN_CHUNKS = tuple((s, min(512, NP_IN - s)) for s in range(0, NP_IN, 512))


def _inproj_kernel(*refs, with_residual):
    if with_residual:
        x1_ref, f_ref, gate_ref, g_ref, sh_ref, sc_ref, w_ref, o_ref, x_ref = refs
        x = x1_ref[...] + gate_ref[...] * f_ref[...]
        x_ref[...] = x
    else:
        xin_ref, g_ref, sh_ref, sc_ref, w_ref, o_ref = refs
        x = xin_ref[...]
    h = _modulated_norm(x, g_ref[...], sc_ref[...], sh_ref[...]).astype(BF16)
    for s, w in _IN_CHUNKS:
        o_ref[:, s:s + w] = _dot(h, w_ref[:, s:s + w])


def _in_projection(x2, g, mod3, w_packed, layer, bsz, seq, residual=None):
    t = x2.shape[0]
    tpb = seq // TM_PROJ
    rowblk = pl.BlockSpec((TM_PROJ, D_MODEL), lambda i: (i, 0))
    common = [_resident((1, D_MODEL), layer), _mod_spec(layer, bsz, 0, tpb), _mod_spec(layer, bsz, 1, tpb),
              _resident((D_MODEL, NP_IN), layer)]
    proj_spec = pl.BlockSpec((TM_PROJ, NP_IN), lambda i: (i, 0))
    proj_shape = jax.ShapeDtypeStruct((t, NP_IN), F32)
    if residual is None:
        in_specs = [rowblk] + common
        args = [x2, g, mod3, mod3, w_packed]
        out_specs, out_shape = proj_spec, proj_shape
    else:
        f = residual
        in_specs = [rowblk, rowblk, _mod_spec(layer - 1, bsz, 5, tpb)] + common
        args = [x2, f, mod3, g, mod3, mod3, w_packed]
        out_specs = [proj_spec, rowblk]
        out_shape = [proj_shape, jax.ShapeDtypeStruct((t, D_MODEL), F32)]
    return pl.pallas_call(
        functools.partial(_inproj_kernel, with_residual=residual is not None),
        grid=(t // TM_PROJ,),
        in_specs=in_specs, out_specs=out_specs, out_shape=out_shape,
        compiler_params=_cparams(("arbitrary",)),
        name="in_projection",
    )(*args)


S5_TS = 256
S5_NROW = SUBLANE
S5_SW = S5_NSTATE // S5_NROW
S5_LT = S5_SW // LANE
S5_PITCH = S5_TS + SUBLANE
S5_NOUT = 2
S5_RPO = S5_NROW // S5_NOUT


def _s5_kernel(u_ref, bwr_ref, bwi_ref, cw_ref, lam_ref, d_ref, wg_ref, bg_ref, ng_ref, o_ref,
               slab_ref, st_ref):
    @pl.when(pl.program_id(0) == 0)
    def _():
        st_ref[...] = jnp.zeros_like(st_ref)

    nbatch = u_ref.shape[0]
    for b in range(nbatch):
        ub = u_ref[b].astype(BF16)
        for j in range(S5_NROW):
            uj = ub[:, (j // 2) * LANE:(j // 2 + 1) * LANE]
            parts = (_dot(uj, bwr_ref[j]), _dot(uj, bwi_ref[j]))
            rows = slice(j * S5_PITCH, j * S5_PITCH + S5_TS)
            for c in range(2):
                for q in range(S5_LT):
                    slab_ref[b, c, q, rows, :] = parts[c][:, q * LANE:(q + 1) * LANE]

    lr = [lam_ref[0, :, q * LANE:(q + 1) * LANE] for q in range(S5_LT)]
    li = [lam_ref[1, :, q * LANE:(q + 1) * LANE] for q in range(S5_LT)]

    def step(k, carry):
        rows = pl.ds(k, S5_NROW, stride=S5_PITCH)
        out = []
        for b in range(nbatch):
            for q in range(S5_LT):
                hr, hi = carry[b * S5_LT + q]
                nr = lr[q] * hr - li[q] * hi + slab_ref[b, 0, q, rows, :]
                ni = lr[q] * hi + li[q] * hr + slab_ref[b, 1, q, rows, :]
                slab_ref[b, 0, q, rows, :] = nr
                slab_ref[b, 1, q, rows, :] = ni
                out.append((nr, ni))
        return tuple(out)

    init = tuple((st_ref[b, 0, q], st_ref[b, 1, q]) for b in range(nbatch) for q in range(S5_LT))
    final = lax.fori_loop(0, S5_TS, step, init, unroll=2)
    for b in range(nbatch):
        for q in range(S5_LT):
            st_ref[b, 0, q] = final[b * S5_LT + q][0]
            st_ref[b, 1, q] = final[b * S5_LT + q][1]

    for b in range(nbatch):
        ys = []
        for p in range(S5_NOUT):
            cols = []
            for c in range(2):
                for j in range(p * S5_RPO, (p + 1) * S5_RPO):
                    rows = slice(j * S5_PITCH, j * S5_PITCH + S5_TS)
                    cols += [slab_ref[b, c, q, rows, :] for q in range(S5_LT)]
            ys.append(_dot(jnp.concatenate(cols, axis=1).astype(BF16), cw_ref[p]))
        y = jnp.concatenate(ys, axis=1) + d_ref[...] * u_ref[b]
        z = jax.nn.gelu(y).astype(BF16)
        zz = _dot(z, wg_ref[...]) + bg_ref[...]
        o = zz[:, :S5_WIDTH] * jax.nn.sigmoid(zz[:, S5_WIDTH:])
        o = o * lax.rsqrt(jnp.mean(o * o, axis=-1, keepdims=True) + EPS) * ng_ref[...]
        o_ref[b] = o.astype(BF16)


def _s5_mixer(proj, p, layer, bsz, seq):
    t = bsz * seq
    ublk = COL_S5 // S5_WIDTH
    res = functools.partial(_resident, layer=layer)
    out = pl.pallas_call(
        _s5_kernel,
        grid=(seq // S5_TS,),
        in_specs=[
            pl.BlockSpec((bsz, S5_TS, S5_WIDTH), lambda s: (0, s, ublk)),
            res((S5_NROW, LANE, S5_SW)),
            res((S5_NROW, LANE, S5_SW)),
            res((S5_NOUT, 2 * S5_RPO * S5_SW, S5_WIDTH // S5_NOUT)),
            res((2, S5_NROW, S5_SW)),
            res((1, S5_WIDTH)),
            res((S5_WIDTH, 2 * S5_WIDTH)),
            res((1, 2 * S5_WIDTH)),
            res((1, S5_WIDTH)),
        ],
        out_specs=pl.BlockSpec((bsz, S5_TS, S5_WIDTH), lambda s: (0, s, 0)),
        out_shape=jax.ShapeDtypeStruct((bsz, seq, S5_WIDTH), BF16),
        scratch_shapes=[
            pltpu.VMEM((bsz, 2, S5_LT, S5_NROW * S5_PITCH, LANE), F32),
            pltpu.VMEM((bsz, 2, S5_LT, S5_NROW, LANE), F32),
        ],
        compiler_params=_cparams(("arbitrary",)),
        name="s5_mixer",
    )(proj.reshape(bsz, seq, -1), p["bw_re"], p["bw_im"], p["cw"], p["lam"],
      p["d"], p["w_glu"], p["b_glu"], p["norm_g"])
    return out.reshape(t, S5_WIDTH)


def _s5_params(a_re, a_im, log_dt, b_re, b_im, c_re, c_im, d_skip, w_glu, b_glu, norm_g):
    dt = jnp.exp(log_dt)[:, None]
    mag = jnp.exp(a_re * dt)
    abar_re = mag * jnp.cos(a_im * dt)
    abar_im = mag * jnp.sin(a_im * dt)
    nr = abar_re - 1.0
    ni = abar_im
    den = a_re * a_re + a_im * a_im
    f_re = (nr * a_re + ni * a_im) / den
    f_im = (ni * a_re - nr * a_im) / den
    bbar_re = f_re[..., None] * b_re - f_im[..., None] * b_im
    bbar_im = f_re[..., None] * b_im + f_im[..., None] * b_re
    eye = jnp.eye(S5_GROUPS, dtype=F32)
    bbd = lambda m: (jnp.transpose(m, (0, 2, 1))[:, :, None, :] * eye[:, None, :, None]
                     ).reshape(S5_WIDTH, S5_NSTATE)
    cbd = lambda m: (jnp.transpose(m, (0, 2, 1))[:, :, None, :] * eye[:, None, :, None]
                     ).reshape(S5_NSTATE, S5_WIDTH)
    bw = lambda m: jnp.stack([m[(j // 2) * LANE:(j // 2 + 1) * LANE, j * S5_SW:(j + 1) * S5_SW]
                              for j in range(S5_NROW)]).astype(BF16)
    cre, cim = cbd(c_re), cbd(c_im)
    ns, nc = S5_RPO * S5_SW, S5_WIDTH // S5_NOUT
    cw = jnp.stack([jnp.concatenate([cre[p * ns:(p + 1) * ns, p * nc:(p + 1) * nc],
                                     -cim[p * ns:(p + 1) * ns, p * nc:(p + 1) * nc]], axis=0)
                    for p in range(S5_NOUT)]).astype(BF16)
    lam = jnp.stack([abar_re.reshape(S5_NROW, S5_SW), abar_im.reshape(S5_NROW, S5_SW)])
    return {
        "bw_re": bw(bbd(bbar_re)), "bw_im": bw(bbd(bbar_im)), "cw": cw, "lam": lam,
        "d": d_skip.reshape(1, S5_WIDTH), "w_glu": w_glu.astype(BF16),
        "b_glu": b_glu.reshape(1, -1), "norm_g": norm_g.reshape(1, S5_WIDTH),
    }


LRU_LT = LRU_WIDTH // LANE
LRU_PITCH = SEG + SUBLANE


def _lru_kernel(x_ref, gb_ref, cw_ref, cb_ref, wa_ref, ba_ref, wx_ref, bx_ref, lam_ref, ng_ref,
                o_ref, xe_ref, a_ref, b_ref, hc_ref):
    first = pl.program_id(1) == 0

    @pl.when(first)
    def _():
        xe_ref[0:SUBLANE, :] = jnp.zeros((SUBLANE, LRU_WIDTH), F32)
        hc_ref[...] = jnp.zeros_like(hc_ref)

    @pl.when(jnp.logical_not(first))
    def _():
        xe_ref[0:SUBLANE, :] = xe_ref[TS:TS + SUBLANE, :]

    xe_ref[SUBLANE:TS + SUBLANE, :] = x_ref[...]
    xc = cb_ref[...]
    for w in range(CONV_WIDTH):
        off = SUBLANE - (CONV_WIDTH - 1) + w
        xc = xc + cw_ref[w:w + 1, :] * xe_ref[off:off + TS, :]
    xb = xc.astype(BF16)
    r = jax.nn.sigmoid(_dot(xb, wa_ref[...]) + ba_ref[...])
    gi = jax.nn.sigmoid(_dot(xb, wx_ref[...]) + bx_ref[...])
    log_a = LRU_C * r * jax.nn.log_sigmoid(lam_ref[...])
    a = jnp.exp(log_a)
    one_minus_a2 = -jnp.tanh(log_a) * (1.0 + a * a)
    bt = jnp.sqrt(jnp.maximum(one_minus_a2, 0.0)) * gi * xc
    for j in range(NSEG):
        src = slice(j * SEG, (j + 1) * SEG)
        dst = slice(j * LRU_PITCH, j * LRU_PITCH + SEG)
        for q in range(LRU_LT):
            a_ref[q, dst, :] = a[src, q * LANE:(q + 1) * LANE]
            b_ref[q, dst, :] = bt[src, q * LANE:(q + 1) * LANE]

    def local_scan(k, carry):
        rows = pl.ds(k, NSEG, stride=LRU_PITCH)
        out = []
        for q in range(LRU_LT):
            h, ap = carry[q]
            aq = a_ref[q, rows, :]
            h = aq * h + b_ref[q, rows, :]
            ap = ap * aq
            b_ref[q, rows, :] = h
            a_ref[q, rows, :] = ap
            out.append((h, ap))
        return tuple(out)

    init = (jnp.zeros((NSEG, LANE), F32), jnp.ones((NSEG, LANE), F32))
    ends = lax.fori_loop(0, SEG, local_scan, tuple(init for _ in range(LRU_LT)))
    e = jnp.concatenate([ends[q][0] for q in range(LRU_LT)], axis=1)
    ap = jnp.concatenate([ends[q][1] for q in range(LRU_LT)], axis=1)

    s = hc_ref[...]
    g = gb_ref[...]
    for j in range(NSEG):
        rs = slice(j * SEG, (j + 1) * SEG)
        ps = slice(j * LRU_PITCH, j * LRU_PITCH + SEG)
        hl = jnp.concatenate([b_ref[q, ps, :] for q in range(LRU_LT)], axis=1)
        al = jnp.concatenate([a_ref[q, ps, :] for q in range(LRU_LT)], axis=1)
        h = hl + al * s
        o = h * jax.nn.gelu(g[rs, :])
        o = o * lax.rsqrt(jnp.mean(o * o, axis=-1, keepdims=True) + EPS) * ng_ref[...]
        o_ref[rs, :] = o.astype(BF16)
        s = e[j:j + 1] + ap[j:j + 1] * s
    hc_ref[...] = s


def _lru_mixer(proj, p, layer, bsz, seq):
    t = bsz * seq
    nb = seq // TS
    res = functools.partial(_resident, layer=layer)
    vec = res((1, LRU_WIDTH))
    return pl.pallas_call(
        _lru_kernel,
        grid=(bsz, nb),
        in_specs=[
            pl.BlockSpec((TS, LRU_WIDTH), lambda b, s: (b * nb + s, COL_LRU_X // LRU_WIDTH)),
            pl.BlockSpec((TS, LRU_WIDTH), lambda b, s: (b * nb + s, COL_LRU_G // LRU_WIDTH)),
            res((CONV_WIDTH, LRU_WIDTH)), vec,
            res((LRU_WIDTH, LRU_WIDTH)), vec,
            res((LRU_WIDTH, LRU_WIDTH)), vec,
            vec, vec,
        ],
        out_specs=pl.BlockSpec((TS, LRU_WIDTH), lambda b, s: (b * nb + s, 0)),
        out_shape=jax.ShapeDtypeStruct((t, LRU_WIDTH), BF16),
        scratch_shapes=[
            pltpu.VMEM((TS + SUBLANE, LRU_WIDTH), F32),
            pltpu.VMEM((LRU_LT, NSEG * LRU_PITCH, LANE), F32),
            pltpu.VMEM((LRU_LT, NSEG * LRU_PITCH, LANE), F32),
            pltpu.VMEM((1, LRU_WIDTH), F32),
        ],
        compiler_params=_cparams(("arbitrary", "arbitrary")),
        name="rglru_mixer",
    )(proj, proj, p["conv_w"], p["conv_b"], p["wa"], p["ba"], p["wx"], p["bx"], p["lam"], p["norm_g"])


def _lru_params(conv_w, conv_b, w_a, b_a, w_x, b_x, lam, norm_g):
    eye = jnp.eye(LRU_HEADS, dtype=F32)
    bd = lambda m: (m[:, :, None, :] * eye[:, None, :, None]).reshape(LRU_WIDTH, LRU_WIDTH).astype(BF16)
    row = lambda v: v.reshape(1, LRU_WIDTH)
    return {"conv_w": conv_w, "conv_b": row(conv_b), "wa": bd(w_a), "ba": row(b_a),
            "wx": bd(w_x), "bx": row(b_x), "lam": row(lam), "norm_g": row(norm_g)}


def _split3(x):
    hi = x.astype(BF16)
    r1 = x - hi.astype(F32)
    mid = r1.astype(BF16)
    lo = (r1 - mid.astype(F32)).astype(BF16)
    return hi, mid, lo


def _gla_kernel(q_ref, k_ref, v_ref, og_ref, gl_ref, wg_ref, bg_ref, hg_ref, o_ref, st_ref):
    @pl.when(pl.program_id(1) == 0)
    def _():
        st_ref[...] = jnp.zeros_like(st_ref)

    c = GLA_CHUNK
    hc = GLA_HEADS * c
    ri = lax.broadcasted_iota(jnp.int32, (c, c), 0)
    ci = lax.broadcasted_iota(jnp.int32, (c, c), 1)
    tri = (ri >= ci).astype(BF16)
    rr = lax.broadcasted_iota(jnp.int32, (hc, hc), 0)
    cc = lax.broadcasted_iota(jnp.int32, (hc, hc), 1)
    causal = jnp.logical_and(rr // c == cc // c, rr % c >= cc % c)
    heads = lambda m, w: jnp.concatenate([m[:, h * w:(h + 1) * w] for h in range(GLA_HEADS)], axis=0)

    for n in range(TS // c):
        rs = slice(n * c, (n + 1) * c)
        logit = _dot(gl_ref[rs, :].astype(BF16), wg_ref[...]) + bg_ref[...]
        la = jax.nn.log_sigmoid(logit) / GLA_TAU
        hi, mid, lo = _split3(la)
        bcum = _dot(tri, hi) + _dot(tri, mid) + _dot(tri, lo)
        b_last = bcum[c - 1:c, :]
        kk = k_ref[rs, :]
        q_dec = q_ref[rs, :] * (GLA_DK ** -0.5) * jnp.exp(bcum)
        k_in = kk * jnp.exp(-bcum)
        k_out = (kk * jnp.exp(b_last - bcum)).astype(BF16)
        gamma = jnp.exp(b_last)
        vb = v_ref[rs, :].astype(BF16)
        qd = q_dec.astype(BF16)

        att = lax.dot_general(heads(qd, GLA_DKP), heads(k_in.astype(BF16), GLA_DKP),
                              (((1,), (1,)), ((), ())), preferred_element_type=F32)
        att = jnp.where(causal, att, 0.0).astype(BF16)
        o_s = _dot(att, heads(vb, GLA_DVP))
        outs = []
        for h in range(GLA_HEADS):
            ks = slice(h * GLA_DKP, (h + 1) * GLA_DKP)
            vs = slice(h * GLA_DVP, (h + 1) * GLA_DVP)
            st = st_ref[h]
            o_h = o_s[h * c:(h + 1) * c, :] + lax.dot_general(
                qd[:, ks], st.astype(BF16), (((1,), (1,)), ((), ())), preferred_element_type=F32)
            delta = lax.dot_general(vb[:, vs], k_out[:, ks], (((0,), (0,)), ((), ())),
                                    preferred_element_type=F32)
            st_ref[h] = gamma[:, ks] * st + delta
            ms = jnp.sum(o_h * o_h, axis=-1, keepdims=True) * (1.0 / GLA_DV)
            outs.append(o_h * lax.rsqrt(ms + EPS) * hg_ref[...])
        o = jnp.concatenate(outs, axis=1)
        og = og_ref[rs, :]
        o_ref[rs, :] = (o * (og * jax.nn.sigmoid(og))).astype(BF16)


def _gla_mixer(proj, p, layer, bsz, seq):
    t = bsz * seq
    nb = seq // TS
    row = lambda b, s: b * nb + s
    res = functools.partial(_resident, layer=layer)
    return pl.pallas_call(
        _gla_kernel,
        grid=(bsz, nb),
        in_specs=[
            pl.BlockSpec((TS, GLA_QW), lambda b, s: (row(b, s), COL_Q // GLA_QW)),
            pl.BlockSpec((TS, GLA_QW), lambda b, s: (row(b, s), COL_K // GLA_QW)),
            pl.BlockSpec((TS, GLA_VW), lambda b, s: (row(b, s), COL_V // GLA_VW)),
            pl.BlockSpec((TS, GLA_VW), lambda b, s: (row(b, s), COL_OG // GLA_VW)),
            pl.BlockSpec((TS, LANE), lambda b, s: (row(b, s), COL_GLOW // LANE)),
            res((LANE, GLA_QW)),
            res((1, GLA_QW)),
            res((1, GLA_DVP)),
        ],
        out_specs=pl.BlockSpec((TS, GLA_VW), lambda b, s: (row(b, s), 0)),
        out_shape=jax.ShapeDtypeStruct((t, GLA_VW), BF16),
        scratch_shapes=[pltpu.VMEM((GLA_HEADS, GLA_DVP, GLA_DKP), F32)],
        compiler_params=_cparams(("arbitrary", "arbitrary")),
        name="gla_mixer",
    )(proj, proj, proj, proj, proj, p["w_g2"], p["b_g2"], p["head_g"])


def _pad_heads(m, d, dp):
    lead = m.shape[:-1]
    m = m.reshape(lead + (GLA_HEADS, d))
    m = jnp.pad(m, [(0, 0)] * len(lead) + [(0, 0), (0, dp - d)])
    return m.reshape(lead + (GLA_HEADS * dp,))


def _gla_params(w_g2, b_g2, head_g):
    w = jnp.pad(_pad_heads(w_g2, GLA_DK, GLA_DKP), ((0, LANE - GLA_GATE_RANK), (0, 0)))
    return {"w_g2": w.astype(BF16),
            "b_g2": _pad_heads(b_g2, GLA_DK, GLA_DKP).reshape(1, GLA_QW),
            "head_g": jnp.pad(head_g, (0, GLA_DVP - GLA_DV)).reshape(1, GLA_DVP)}


def _pack_w_in(w):
    s5 = w[:, 0:512]
    lx = w[:, 512:1280]
    lg = w[:, 1280:2048]
    q = w[:, 2048:2432]
    k = w[:, 2432:2816]
    v = w[:, 2816:3584]
    gl = w[:, 3584:3600]
    og = w[:, 3600:4368]
    cols = [lx, lg, s5, _pad_heads(q, GLA_DK, GLA_DKP), _pad_heads(k, GLA_DK, GLA_DKP),
            _pad_heads(v, GLA_DV, GLA_DVP), _pad_heads(og, GLA_DV, GLA_DVP),
            jnp.pad(gl, ((0, 0), (0, LANE - GLA_GATE_RANK)))]
    return jnp.concatenate(cols, axis=1).astype(BF16)


def _pack_w_out(w):
    gla = w[S5_WIDTH + LRU_WIDTH:].reshape(GLA_HEADS, GLA_DV, D_MODEL)
    gla = jnp.pad(gla, ((0, 0), (0, GLA_DVP - GLA_DV), (0, 0))).reshape(GLA_VW, D_MODEL)
    return jnp.concatenate([w[:S5_WIDTH + LRU_WIDTH], gla], axis=0).astype(BF16)


def _outproj_kernel(ys_ref, yl_ref, yg_ref, x_ref, w_ref, gate_ref, g_ref, sh_ref, sc_ref, *rest,
                    with_router):
    r0, r1 = S5_WIDTH, S5_WIDTH + LRU_WIDTH
    mix = (_dot(ys_ref[...], w_ref[0:r0, :]) + _dot(yl_ref[...], w_ref[r0:r1, :])
           + _dot(yg_ref[...], w_ref[r1:NP_OUT_ROWS, :]))
    x1 = x_ref[...] + gate_ref[...] * mix
    h = _modulated_norm(x1, g_ref[...], sc_ref[...], sh_ref[...])
    if with_router:
        wr_hi_ref, wr_lo_ref, x1_ref, h_ref, lg_ref = rest
        h_ref[...] = h
        hh = h.astype(BF16)
        hm = (h - hh.astype(F32)).astype(BF16)
        lg_ref[...] = (_dot(hh, wr_hi_ref[...]) + _dot(hm, wr_hi_ref[...])
                       + _dot(hh, wr_lo_ref[...]))
    else:
        x1_ref, h_ref = rest
        h_ref[...] = h.astype(BF16)
    x1_ref[...] = x1


def _out_projection(ys, yl, yg, x2, w_packed, g, mod3, layer, bsz, seq, router=None):
    t = x2.shape[0]
    tpb = seq // TM_PROJ
    rowblk = lambda w: pl.BlockSpec((TM_PROJ, w), lambda i: (i, 0))
    in_specs = [rowblk(S5_WIDTH), rowblk(LRU_WIDTH), rowblk(GLA_VW), rowblk(D_MODEL),
                _resident((NP_OUT_ROWS, D_MODEL), layer),
                _mod_spec(layer, bsz, 2, tpb),
                _resident((1, D_MODEL), layer),
                _mod_spec(layer, bsz, 3, tpb), _mod_spec(layer, bsz, 4, tpb)]
    args = [ys, yl, yg, x2, w_packed, mod3, g, mod3, mod3]
    if router is None:
        out_specs = [rowblk(D_MODEL), rowblk(D_MODEL)]
        out_shape = [jax.ShapeDtypeStruct((t, D_MODEL), F32), jax.ShapeDtypeStruct((t, D_MODEL), BF16)]
    else:
        wr = jnp.pad(router, ((0, 0), (0, LANE - N_EXPERTS)))
        wr_hi = wr.astype(BF16)
        wr_lo = (wr - wr_hi.astype(F32)).astype(BF16)
        in_specs += [_resident((D_MODEL, LANE)), _resident((D_MODEL, LANE))]
        args += [wr_hi, wr_lo]
        out_specs = [rowblk(D_MODEL), rowblk(D_MODEL), rowblk(LANE)]
        out_shape = [jax.ShapeDtypeStruct((t, D_MODEL), F32), jax.ShapeDtypeStruct((t, D_MODEL), F32),
                     jax.ShapeDtypeStruct((t, LANE), F32)]
    return pl.pallas_call(
        functools.partial(_outproj_kernel, with_router=router is not None),
        grid=(t // TM_PROJ,),
        in_specs=in_specs, out_specs=out_specs, out_shape=out_shape,
        compiler_params=_cparams(("arbitrary",)),
        name="out_projection",
    )(*args)


def _ffn_kernel(se_ref, sn_ref, sx_ref, x_ref, w1_ref, w3_ref, w2_ref, o_ref, w1b_ref, w3b_ref, w2b_ref,
                *, q_tiles):
    s = pl.program_id(0)
    n = sn_ref[s]

    @pl.when(pl.program_id(1) == 0)
    def _():
        o_ref[...] = jnp.zeros_like(o_ref)

    @pl.when(n > 0)
    def _():
        w1b_ref[...] = w1_ref[...].astype(BF16)
        w3b_ref[...] = w3_ref[...].astype(BF16)
        w2b_ref[...] = w2_ref[...].astype(BF16)

    def swiglu_rows(first, count):
        rows = slice(first * FFN_TM, (first + count) * FFN_TM)
        xq = x_ref[rows, :]
        h1 = _dot(xq, w1b_ref[...])
        h3 = _dot(xq, w3b_ref[...])
        a = (h1 * jax.nn.sigmoid(h1) * h3).astype(BF16)
        o_ref[rows, :] += _dot(a, w2b_ref[...])

    for g0 in range(0, q_tiles, FFN_GROUP):
        gsz = min(FFN_GROUP, q_tiles - g0)
        if gsz > 1:
            pl.when(n >= g0 + gsz)(functools.partial(swiglu_rows, g0, gsz))
        for q in range(g0, g0 + gsz):
            cond = jnp.logical_and(q < n, n < g0 + gsz) if gsz > 1 else q < n
            pl.when(cond)(functools.partial(swiglu_rows, q, 1))


def _ffn(tables, xs, w1, w3, w2, q_tiles):
    r, d = xs.shape
    dff = w1.shape[2]
    nf = dff // FFN_FC
    rows = q_tiles * FFN_TM
    ns = r // rows

    def fidx(f, s, sn):
        return jnp.where(sn[s] > 0, f, nf - 1)

    grid_spec = pltpu.PrefetchScalarGridSpec(
        num_scalar_prefetch=3,
        grid=(ns, nf),
        in_specs=[
            pl.BlockSpec((rows, d), lambda s, f, se, sn, sx: (sx[s], 0),
                         pipeline_mode=pl.Buffered(1 if q_tiles > FFN_Q_DENSE else 2)),
            pl.BlockSpec((None, d, FFN_FC), lambda s, f, se, sn, sx: (se[s], 0, fidx(f, s, sn))),
            pl.BlockSpec((None, d, FFN_FC), lambda s, f, se, sn, sx: (se[s], 0, fidx(f, s, sn))),
            pl.BlockSpec((None, FFN_FC, d), lambda s, f, se, sn, sx: (se[s], fidx(f, s, sn), 0)),
        ],
        out_specs=pl.BlockSpec((rows, d), lambda s, f, se, sn, sx: (s, 0),
                               pipeline_mode=pl.Buffered(1)),
        scratch_shapes=[pltpu.VMEM((d, FFN_FC), BF16), pltpu.VMEM((d, FFN_FC), BF16),
                        pltpu.VMEM((FFN_FC, d), BF16)],
    )
    return pl.pallas_call(
        functools.partial(_ffn_kernel, q_tiles=q_tiles),
        grid_spec=grid_spec,
        out_shape=jax.ShapeDtypeStruct((r, d), F32),
        compiler_params=_cparams(("arbitrary", "arbitrary")),
        name="ffn_swiglu",
    )(*tables, xs, w1, w3, w2)


def _dense_tables(t):
    ns = t // (FFN_Q_DENSE * FFN_TM)
    return (jnp.zeros((ns,), jnp.int32), jnp.full((ns,), FFN_Q_DENSE, jnp.int32),
            jnp.arange(ns, dtype=jnp.int32))


def _route_kernel(lg_ref, o_ref, cnt_ref, carry_ref):
    @pl.when(pl.program_id(0) == 0)
    def _():
        carry_ref[...] = jnp.zeros_like(carry_ref)

    tm = TM_ROUTE
    lane = lax.broadcasted_iota(jnp.int32, (tm, LANE), 1)
    neg = jnp.float32(-jnp.inf)
    l = jnp.where(lane < N_EXPERTS, lg_ref[...], neg)
    m1 = jnp.max(l, axis=1, keepdims=True)
    i1 = jnp.min(jnp.where(l == m1, lane, LANE), axis=1, keepdims=True)
    oh1 = lane == i1
    l2 = jnp.where(oh1, neg, l)
    m2 = jnp.max(l2, axis=1, keepdims=True)
    i2 = jnp.min(jnp.where(l2 == m2, lane, LANE), axis=1, keepdims=True)
    oh2 = lane == i2
    e = jnp.exp(m2 - m1)
    den = 1.0 + e
    g1 = 1.0 / den
    g2 = e / den
    oh = jnp.where(jnp.logical_or(oh1, oh2), 1.0, 0.0)
    ri = lax.broadcasted_iota(jnp.int32, (tm, tm), 0)
    ci = lax.broadcasted_iota(jnp.int32, (tm, tm), 1)
    before = (ri > ci).astype(BF16)
    cum = _dot(before, oh.astype(BF16)) + carry_ref[...]
    r1 = jnp.sum(jnp.where(oh1, cum, 0.0), axis=1, keepdims=True)
    r2 = jnp.sum(jnp.where(oh2, cum, 0.0), axis=1, keepdims=True)
    carry_ref[...] = carry_ref[...] + jnp.sum(oh, axis=0, keepdims=True)
    cnt_ref[...] = carry_ref[...]
    out = jnp.where(lane == 0, i1.astype(F32), 0.0)
    out = jnp.where(lane == 1, i2.astype(F32), out)
    out = jnp.where(lane == 2, g1, out)
    out = jnp.where(lane == 3, g2, out)
    out = jnp.where(lane == 4, r1, out)
    out = jnp.where(lane == 5, r2, out)
    o_ref[...] = out


def _route(logits):
    t = logits.shape[0]
    return pl.pallas_call(
        _route_kernel,
        grid=(t // TM_ROUTE,),
        in_specs=[pl.BlockSpec((TM_ROUTE, LANE), lambda i: (i, 0))],
        out_specs=[pl.BlockSpec((TM_ROUTE, LANE), lambda i: (i, 0)),
                   pl.BlockSpec((1, LANE), lambda i: (0, 0))],
        out_shape=[jax.ShapeDtypeStruct((t, LANE), F32), jax.ShapeDtypeStruct((1, LANE), F32)],
        scratch_shapes=[pltpu.VMEM((1, LANE), F32)],
        compiler_params=_cparams(("arbitrary",)),
        name="route_top2",
    )(logits)


def _expert_tables(route, counts, t):
    qt = FFN_Q_EXPERT
    rows = qt * FFN_TM
    max_tiles = (2 * t + FFN_TM - 1) // FFN_TM + N_EXPERTS
    ns = (max_tiles + N_EXPERTS * (qt - 1)) // qt
    i1 = route[:, 0].astype(jnp.int32)
    i2 = route[:, 1].astype(jnp.int32)
    r1 = route[:, 4].astype(jnp.int32)
    r2 = route[:, 5].astype(jnp.int32)
    cnt = counts[0, :N_EXPERTS].astype(jnp.int32)
    tiles = (cnt + FFN_TM - 1) // FFN_TM
    supers = (tiles + qt - 1) // qt
    super_end = jnp.cumsum(supers)
    super_start = super_end - supers
    n_used = super_end[-1]
    ids = jnp.arange(ns, dtype=jnp.int32)
    sx = jnp.minimum(ids, n_used - 1)
    se = jnp.minimum(jnp.sum((sx[:, None] >= super_end[None, :]).astype(jnp.int32), axis=1),
                     N_EXPERTS - 1)
    sn = jnp.clip(tiles[se] - (sx - super_start[se]) * qt, 0, qt)
    sn = jnp.where(ids < n_used, sn, 0).astype(jnp.int32)
    row_off = super_start * rows
    pos1 = row_off[i1] + r1
    pos2 = row_off[i2] + r2
    tok = jnp.arange(t, dtype=jnp.int32)
    src = jnp.zeros((ns * rows,), jnp.int32).at[jnp.concatenate([pos1, pos2])].set(
        jnp.concatenate([tok, tok]))
    gpt = rows // TM_GATHER
    gid = jnp.arange(ns * gpt, dtype=jnp.int32)
    gv = ((gid % gpt) * TM_GATHER < sn[gid // gpt] * FFN_TM).astype(jnp.int32)
    return (se, sn, sx), pos1, pos2, src, gv


def _row_copy(src_hbm, row, dst, slot, sem):
    return pltpu.make_async_copy(src_hbm.at[pl.ds(row, 1), :], dst.at[pl.ds(slot, 1), :], sem)


def _tile_copy(src_hbm, dst, sem):
    return pltpu.make_async_copy(src_hbm.at[pl.ds(0, dst.shape[0]), :], dst, sem)


def _gather_kernel(src_ref, gv_ref, h_hbm, o_ref, buf_ref, sem):
    i = pl.program_id(0)
    slot = i % 2

    def issue(tile, sl):
        base = tile * TM_GATHER

        def body(r, c):
            _row_copy(h_hbm, src_ref[base + r], buf_ref.at[sl], r, sem.at[sl]).start()
            return c

        lax.fori_loop(0, TM_GATHER, body, 0, unroll=8)

    @pl.when(jnp.logical_and(i == 0, gv_ref[0] == 1))
    def _():
        issue(0, 0)

    nxt = jnp.minimum(i + 1, pl.num_programs(0) - 1)

    @pl.when(jnp.logical_and(i + 1 < pl.num_programs(0), gv_ref[nxt] == 1))
    def _():
        issue(i + 1, 1 - slot)

    @pl.when(gv_ref[i] == 1)
    def _():
        _tile_copy(h_hbm, buf_ref.at[slot], sem.at[slot]).wait()
        o_ref[...] = buf_ref[slot].astype(BF16)

    @pl.when(gv_ref[i] == 0)
    def _():
        o_ref[...] = jnp.zeros_like(o_ref)


def _gather_rows(src, gv, h):
    nt = gv.shape[0]
    d = h.shape[1]
    grid_spec = pltpu.PrefetchScalarGridSpec(
        num_scalar_prefetch=2,
        grid=(nt,),
        in_specs=[pl.BlockSpec(memory_space=pl.ANY)],
        out_specs=pl.BlockSpec((TM_GATHER, d), lambda i, src, gv: (i, 0)),
        scratch_shapes=[pltpu.VMEM((2, TM_GATHER, d), F32), pltpu.SemaphoreType.DMA((2,))],
    )
    return pl.pallas_call(
        _gather_kernel,
        grid_spec=grid_spec,
        out_shape=jax.ShapeDtypeStruct((nt * TM_GATHER, d), BF16),
        compiler_params=_cparams(("arbitrary",)),
        name="expert_gather",
    )(src, gv, h)


def _combine_kernel(p1_ref, p2_ref, y_hbm, x1_ref, gate_ref, rt_ref, fg_ref, o_ref, ya_ref, yb_ref, sem):
    base = pl.program_id(0) * TM_COMB

    def issue(r, c):
        _row_copy(y_hbm, p1_ref[base + r], ya_ref, r, sem.at[0]).start()
        _row_copy(y_hbm, p2_ref[base + r], yb_ref, r, sem.at[1]).start()
        return c

    lax.fori_loop(0, TM_COMB, issue, 0, unroll=8)
    _tile_copy(y_hbm, ya_ref, sem.at[0]).wait()
    _tile_copy(y_hbm, yb_ref, sem.at[1]).wait()
    rt = rt_ref[...]
    f = rt[:, 2:3] * ya_ref[...] + rt[:, 3:4] * yb_ref[...]
    x2 = x1_ref[...] + gate_ref[...] * f
    o_ref[...] = x2 * lax.rsqrt(jnp.mean(x2 * x2, axis=-1, keepdims=True) + EPS) * fg_ref[...]


def _combine(pos1, pos2, y, x1, mod3, route, final_g, layer, bsz, seq):
    t, d = x1.shape
    tpb = seq // TM_COMB
    grid_spec = pltpu.PrefetchScalarGridSpec(
        num_scalar_prefetch=2,
        grid=(t // TM_COMB,),
        in_specs=[
            pl.BlockSpec(memory_space=pl.ANY),
            pl.BlockSpec((TM_COMB, d), lambda i, p1, p2: (i, 0)),
            pl.BlockSpec((None, 1, d), lambda i, p1, p2: ((layer * bsz + i // tpb) * N_ADA + 5, 0, 0)),
            pl.BlockSpec((TM_COMB, LANE), lambda i, p1, p2: (i, 0)),
            pl.BlockSpec((1, d), lambda i, p1, p2: (0, 0)),
        ],
        out_specs=pl.BlockSpec((TM_COMB, d), lambda i, p1, p2: (i, 0)),
        scratch_shapes=[pltpu.VMEM((TM_COMB, d), F32), pltpu.VMEM((TM_COMB, d), F32),
                        pltpu.SemaphoreType.DMA((2,))],
    )
    return pl.pallas_call(
        _combine_kernel,
        grid_spec=grid_spec,
        out_shape=jax.ShapeDtypeStruct((t, d), F32),
        compiler_params=_cparams(("arbitrary",)),
        name="expert_combine_norm",
    )(pos1, pos2, y, x1, mod3, route, final_g.reshape(1, d))


def kernel(x, c, ada_w, ada_b, norm_mix_g, norm_ffn_g, w_in, s5_a_re, s5_a_im, s5_log_dt, s5_b_re, s5_b_im, s5_c_re, s5_c_im, s5_d, s5_w_glu, s5_b_glu, s5_norm_g, lru_conv_w, lru_conv_b, lru_w_a, lru_b_a, lru_w_x, lru_b_x, lru_lambda, lru_norm_g, gla_w_g2, gla_b_g2, gla_head_g, w_out, ffn_w1, ffn_w3, ffn_w2, moe_router, moe_w1, moe_w3, moe_w2, final_norm_g):
    bsz, seq, d = x.shape
    depth = ada_w.shape[0]
    t = bsz * seq
    assert depth == 2 and d == D_MODEL and seq % TS == 0 and t % (FFN_Q_DENSE * FFN_TM) == 0
    mod3 = _adaln_mod(c, ada_w, ada_b).reshape(depth * bsz * N_ADA, 1, D_MODEL)
    s5p = jax.vmap(_s5_params)(s5_a_re, s5_a_im, s5_log_dt, s5_b_re, s5_b_im, s5_c_re, s5_c_im, s5_d,
                               s5_w_glu, s5_b_glu, s5_norm_g)
    lrup = jax.vmap(_lru_params)(lru_conv_w, lru_conv_b, lru_w_a, lru_b_a, lru_w_x, lru_b_x,
                                 lru_lambda, lru_norm_g)
    glap = jax.vmap(_gla_params)(gla_w_g2, gla_b_g2, gla_head_g)
    w_in_p = jax.vmap(_pack_w_in)(w_in)
    w_out_p = jax.vmap(_pack_w_out)(w_out)
    g_mix = norm_mix_g.reshape(depth, 1, D_MODEL)
    g_ffn = norm_ffn_g.reshape(depth, 1, D_MODEL)

    xf = x.reshape(t, D_MODEL)
    pending = None
    out = None
    for l in range(depth):
        if pending is None:
            proj = _in_projection(xf, g_mix, mod3, w_in_p, l, bsz, seq)
        else:
            proj, xf = _in_projection(xf, g_mix, mod3, w_in_p, l, bsz, seq, residual=pending)
        y_s5 = _s5_mixer(proj, s5p, l, bsz, seq)
        y_lru = _lru_mixer(proj, lrup, l, bsz, seq)
        y_gla = _gla_mixer(proj, glap, l, bsz, seq)
        if l % 2 == 0:
            xf, h = _out_projection(y_s5, y_lru, y_gla, xf, w_out_p, g_ffn, mod3, l, bsz, seq)
            k = l // 2
            pending = _ffn(_dense_tables(t), h, ffn_w1[k:k + 1], ffn_w3[k:k + 1], ffn_w2[k:k + 1],
                           FFN_Q_DENSE)
        else:
            x1, h, logits = _out_projection(y_s5, y_lru, y_gla, xf, w_out_p, g_ffn, mod3, l, bsz, seq,
                                            router=moe_router[l // 2])
            route, counts = _route(logits)
            tables, pos1, pos2, src, gv = _expert_tables(route, counts, t)
            xs = _gather_rows(src, gv, h)
            y = _ffn(tables, xs, moe_w1[l // 2], moe_w3[l // 2], moe_w2[l // 2], FFN_Q_EXPERT)
            out = _combine(pos1, pos2, y, x1, mod3, route, final_norm_g, l, bsz, seq)
    return out.reshape(bsz, seq, D_MODEL)
```

```python
import functools

import jax
import jax.numpy as jnp
from jax import lax
from jax.experimental import pallas as pl
from jax.experimental.pallas import tpu as pltpu

F32 = jnp.float32
BF16 = jnp.bfloat16

D_MODEL = 2048
S5_WIDTH = 512
S5_GROUP = 16
S5_GROUPS = 32
S5_STATE = 64
S5_NSTATE = S5_GROUPS * S5_STATE
LRU_WIDTH = 768
LRU_HEADS = 8
LRU_HEAD_DIM = 96
CONV_WIDTH = 4
LRU_C = 8.0
GLA_WIDTH = 768
GLA_HEADS = 4
GLA_DV = 192
GLA_DK = 96
GLA_GATE_RANK = 16
GLA_TAU = 16.0
GLA_CHUNK = 64
N_EXPERTS = 8
N_ADA = 6
EPS = 1e-6

LANE = 128
SUBLANE = 8
GLA_DKP = 128
GLA_DVP = 256
GLA_QW = GLA_HEADS * GLA_DKP
GLA_VW = GLA_HEADS * GLA_DVP
VMEM_LIMIT = 56 * 1024 * 1024

COL_LRU_X = 0
COL_LRU_G = 768
COL_S5 = 1536
COL_Q = 2048
COL_K = 2560
COL_V = 3072
COL_OG = 4096
COL_GLOW = 5120
NP_IN = 5248
NP_OUT_ROWS = S5_WIDTH + LRU_WIDTH + GLA_VW

TS = 512
NSEG = SUBLANE
SEG = TS // NSEG

TM_PROJ = 256
TM_ROUTE = 512
TM_COMB = 256
TM_GATHER = 512

FFN_TM = 256
FFN_GROUP = 4
FFN_FC = 256
FFN_Q_DENSE = 8
FFN_Q_EXPERT = 10


def _cparams(sem, vmem=VMEM_LIMIT):
    return pltpu.CompilerParams(dimension_semantics=sem, vmem_limit_bytes=vmem)


def _dot(a, b):
    return jnp.dot(a, b, preferred_element_type=F32)


def _resident(shape, layer=None):
    nd = len(shape)
    if layer is None:
        return pl.BlockSpec(shape, lambda *_: (0,) * nd, pipeline_mode=pl.Buffered(1))
    return pl.BlockSpec((None,) + tuple(shape), lambda *_: (layer,) + (0,) * nd,
                        pipeline_mode=pl.Buffered(1))


def _mod_kernel(c_ref, w_ref, b_ref, o_ref):
    c = c_ref[...]
    cs = (c * jax.nn.sigmoid(c)).astype(BF16)
    o_ref[...] = _dot(cs, w_ref[...].astype(BF16)) + b_ref[...]


def _adaln_mod(c, ada_w, ada_b):
    depth, d, n = ada_w.shape
    bsz = c.shape[0]
    tn = 1024
    return pl.pallas_call(
        _mod_kernel,
        grid=(depth, n // tn),
        in_specs=[
            pl.BlockSpec((bsz, d), lambda l, j: (0, 0)),
            pl.BlockSpec((None, d, tn), lambda l, j: (l, 0, j)),
            pl.BlockSpec((None, 1, tn), lambda l, j: (l, 0, j)),
        ],
        out_specs=pl.BlockSpec((None, bsz, tn), lambda l, j: (l, 0, j)),
        out_shape=jax.ShapeDtypeStruct((depth, bsz, n), F32),
        compiler_params=_cparams(("arbitrary", "arbitrary")),
        name="adaln_mod",
    )(c, ada_w, ada_b.reshape(depth, 1, n))


def _mod_spec(layer, bsz, which, tiles_per_batch):
    def imap(i, *_):
        return ((layer * bsz + i // tiles_per_batch) * N_ADA + which, 0, 0)
    return pl.BlockSpec((None, 1, D_MODEL), imap)


def _modulated_norm(x, g, scale, shift):
    ms = jnp.mean(x * x, axis=-1, keepdims=True)
    h = x * lax.rsqrt(ms + EPS) * g
    return h * (1.0 + scale) + shift


_IN_CHUNKS = tuple((s, min(512, NP_IN - s)) for s in range(0, NP_IN, 512))


def _inproj_kernel(*refs, with_residual):
    if with_residual:
        x1_ref, f_ref, gate_ref, g_ref, sh_ref, sc_ref, w_ref, o_ref, x_ref = refs
        x = x1_ref[...] + gate_ref[...] * f_ref[...]
        x_ref[...] = x
    else:
        xin_ref, g_ref, sh_ref, sc_ref, w_ref, o_ref = refs
        x = xin_ref[...]
    h = _modulated_norm(x, g_ref[...], sc_ref[...], sh_ref[...]).astype(BF16)
    for s, w in _IN_CHUNKS:
        o_ref[:, s:s + w] = _dot(h, w_ref[:, s:s + w])


def _in_projection(x2, g, mod3, w_packed, layer, bsz, seq, residual=None):
    t = x2.shape[0]
    tpb = seq // TM_PROJ
    rowblk = pl.BlockSpec((TM_PROJ, D_MODEL), lambda i: (i, 0))
    common = [_resident((1, D_MODEL), layer), _mod_spec(layer, bsz, 0, tpb), _mod_spec(layer, bsz, 1, tpb),
              _resident((D_MODEL, NP_IN), layer)]
    proj_spec = pl.BlockSpec((TM_PROJ, NP_IN), lambda i: (i, 0))
    proj_shape = jax.ShapeDtypeStruct((t, NP_IN), F32)
    if residual is None:
        in_specs = [rowblk] + common
        args = [x2, g, mod3, mod3, w_packed]
        out_specs, out_shape = proj_spec, proj_shape
    else:
        f = residual
        in_specs = [rowblk, rowblk, _mod_spec(layer - 1, bsz, 5, tpb)] + common
        args = [x2, f, mod3, g, mod3, mod3, w_packed]
        out_specs = [proj_spec, rowblk]
        out_shape = [proj_shape, jax.ShapeDtypeStruct((t, D_MODEL), F32)]
    return pl.pallas_call(
        functools.partial(_inproj_kernel, with_residual=residual is not None),
        grid=(t // TM_PROJ,),
        in_specs=in_specs, out_specs=out_specs, out_shape=out_shape,
        compiler_params=_cparams(("arbitrary",)),
        name="in_projection",
    )(*args)


S5_TS = 256
S5_NROW = SUBLANE
S5_SW = S5_NSTATE // S5_NROW
S5_LT = S5_SW // LANE
S5_PITCH = S5_TS + SUBLANE
S5_NOUT = 2
S5_RPO = S5_NROW // S5_NOUT


def _s5_kernel(u_ref, bwr_ref, bwi_ref, cw_ref, lam_ref, d_ref, wg_ref, bg_ref, ng_ref, o_ref,
               slab_ref, st_ref):
    @pl.when(pl.program_id(0) == 0)
    def _():
        st_ref[...] = jnp.zeros_like(st_ref)

    nbatch = u_ref.shape[0]
    for b in range(nbatch):
        ub = u_ref[b].astype(BF16)
        for j in range(S5_NROW):
            uj = ub[:, (j // 2) * LANE:(j // 2 + 1) * LANE]
            parts = (_dot(uj, bwr_ref[j]), _dot(uj, bwi_ref[j]))
            rows = slice(j * S5_PITCH, j * S5_PITCH + S5_TS)
            for c in range(2):
                for q in range(S5_LT):
                    slab_ref[b, c, q, rows, :] = parts[c][:, q * LANE:(q + 1) * LANE]

    lr = [lam_ref[0, :, q * LANE:(q + 1) * LANE] for q in range(S5_LT)]
    li = [lam_ref[1, :, q * LANE:(q + 1) * LANE] for q in range(S5_LT)]

    def step(k, carry):
        rows = pl.ds(k, S5_NROW, stride=S5_PITCH)
        out = []
        for b in range(nbatch):
            for q in range(S5_LT):
                hr, hi = carry[b * S5_LT + q]
                nr = lr[q] * hr - li[q] * hi + slab_ref[b, 0, q, rows, :]
                ni = lr[q] * hi + li[q] * hr + slab_ref[b, 1, q, rows, :]
                slab_ref[b, 0, q, rows, :] = nr
                slab_ref[b, 1, q, rows, :] = ni
                out.append((nr, ni))
        return tuple(out)

    init = tuple((st_ref[b, 0, q], st_ref[b, 1, q]) for b in range(nbatch) for q in range(S5_LT))
    final = lax.fori_loop(0, S5_TS, step, init, unroll=2)
    for b in range(nbatch):
        for q in range(S5_LT):
            st_ref[b, 0, q] = final[b * S5_LT + q][0]
            st_ref[b, 1, q] = final[b * S5_LT + q][1]

    for b in range(nbatch):
        ys = []
        for p in range(S5_NOUT):
            cols = []
            for c in range(2):
                for j in range(p * S5_RPO, (p + 1) * S5_RPO):
                    rows = slice(j * S5_PITCH, j * S5_PITCH + S5_TS)
                    cols += [slab_ref[b, c, q, rows, :] for q in range(S5_LT)]
            ys.append(_dot(jnp.concatenate(cols, axis=1).astype(BF16), cw_ref[p]))
        y = jnp.concatenate(ys, axis=1) + d_ref[...] * u_ref[b]
        z = jax.nn.gelu(y).astype(BF16)
        zz = _dot(z, wg_ref[...]) + bg_ref[...]
        o = zz[:, :S5_WIDTH] * jax.nn.sigmoid(zz[:, S5_WIDTH:])
        o = o * lax.rsqrt(jnp.mean(o * o, axis=-1, keepdims=True) + EPS) * ng_ref[...]
        o_ref[b] = o.astype(BF16)


def _s5_mixer(proj, p, layer, bsz, seq):
    t = bsz * seq
    ublk = COL_S5 // S5_WIDTH
    res = functools.partial(_resident, layer=layer)
    out = pl.pallas_call(
        _s5_kernel,
        grid=(seq // S5_TS,),
        in_specs=[
            pl.BlockSpec((bsz, S5_TS, S5_WIDTH), lambda s: (0, s, ublk)),
            res((S5_NROW, LANE, S5_SW)),
            res((S5_NROW, LANE, S5_SW)),
            res((S5_NOUT, 2 * S5_RPO * S5_SW, S5_WIDTH // S5_NOUT)),
            res((2, S5_NROW, S5_SW)),
            res((1, S5_WIDTH)),
            res((S5_WIDTH, 2 * S5_WIDTH)),
            res((1, 2 * S5_WIDTH)),
            res((1, S5_WIDTH)),
        ],
        out_specs=pl.BlockSpec((bsz, S5_TS, S5_WIDTH), lambda s: (0, s, 0)),
        out_shape=jax.ShapeDtypeStruct((bsz, seq, S5_WIDTH), BF16),
        scratch_shapes=[
            pltpu.VMEM((bsz, 2, S5_LT, S5_NROW * S5_PITCH, LANE), F32),
            pltpu.VMEM((bsz, 2, S5_LT, S5_NROW, LANE), F32),
        ],
        compiler_params=_cparams(("arbitrary",)),
        name="s5_mixer",
    )(proj.reshape(bsz, seq, -1), p["bw_re"], p["bw_im"], p["cw"], p["lam"],
      p["d"], p["w_glu"], p["b_glu"], p["norm_g"])
    return out.reshape(t, S5_WIDTH)


def _s5_params(a_re, a_im, log_dt, b_re, b_im, c_re, c_im, d_skip, w_glu, b_glu, norm_g):
    dt = jnp.exp(log_dt)[:, None]
    mag = jnp.exp(a_re * dt)
    abar_re = mag * jnp.cos(a_im * dt)
    abar_im = mag * jnp.sin(a_im * dt)
    nr = abar_re - 1.0
    ni = abar_im
    den = a_re * a_re + a_im * a_im
    f_re = (nr * a_re + ni * a_im) / den
    f_im = (ni * a_re - nr * a_im) / den
    bbar_re = f_re[..., None] * b_re - f_im[..., None] * b_im
    bbar_im = f_re[..., None] * b_im + f_im[..., None] * b_re
    eye = jnp.eye(S5_GROUPS, dtype=F32)
    bbd = lambda m: (jnp.transpose(m, (0, 2, 1))[:, :, None, :] * eye[:, None, :, None]
                     ).reshape(S5_WIDTH, S5_NSTATE)
    cbd = lambda m: (jnp.transpose(m, (0, 2, 1))[:, :, None, :] * eye[:, None, :, None]
                     ).reshape(S5_NSTATE, S5_WIDTH)
    bw = lambda m: jnp.stack([m[(j // 2) * LANE:(j // 2 + 1) * LANE, j * S5_SW:(j + 1) * S5_SW]
                              for j in range(S5_NROW)]).astype(BF16)
    cre, cim = cbd(c_re), cbd(c_im)
    ns, nc = S5_RPO * S5_SW, S5_WIDTH // S5_NOUT
    cw = jnp.stack([jnp.concatenate([cre[p * ns:(p + 1) * ns, p * nc:(p + 1) * nc],
                                     -cim[p * ns:(p + 1) * ns, p * nc:(p + 1) * nc]], axis=0)
                    for p in range(S5_NOUT)]).astype(BF16)
    lam = jnp.stack([abar_re.reshape(S5_NROW, S5_SW), abar_im.reshape(S5_NROW, S5_SW)])
    return {
        "bw_re": bw(bbd(bbar_re)), "bw_im": bw(bbd(bbar_im)), "cw": cw, "lam": lam,
        "d": d_skip.reshape(1, S5_WIDTH), "w_glu": w_glu.astype(BF16),
        "b_glu": b_glu.reshape(1, -1), "norm_g": norm_g.reshape(1, S5_WIDTH),
    }


LRU_LT = LRU_WIDTH // LANE
LRU_PITCH = SEG + SUBLANE


def _lru_kernel(x_ref, gb_ref, cw_ref, cb_ref, wa_ref, ba_ref, wx_ref, bx_ref, lam_ref, ng_ref,
                o_ref, xe_ref, a_ref, b_ref, hc_ref):
    first = pl.program_id(1) == 0

    @pl.when(first)
    def _():
        xe_ref[0:SUBLANE, :] = jnp.zeros((SUBLANE, LRU_WIDTH), F32)
        hc_ref[...] = jnp.zeros_like(hc_ref)

    @pl.when(jnp.logical_not(first))
    def _():
        xe_ref[0:SUBLANE, :] = xe_ref[TS:TS + SUBLANE, :]

    xe_ref[SUBLANE:TS + SUBLANE, :] = x_ref[...]
    xc = cb_ref[...]
    for w in range(CONV_WIDTH):
        off = SUBLANE - (CONV_WIDTH - 1) + w
        xc = xc + cw_ref[w:w + 1, :] * xe_ref[off:off + TS, :]
    xb = xc.astype(BF16)
    r = jax.nn.sigmoid(_dot(xb, wa_ref[...]) + ba_ref[...])
    gi = jax.nn.sigmoid(_dot(xb, wx_ref[...]) + bx_ref[...])
    log_a = LRU_C * r * jax.nn.log_sigmoid(lam_ref[...])
    a = jnp.exp(log_a)
    one_minus_a2 = -jnp.tanh(log_a) * (1.0 + a * a)
    bt = jnp.sqrt(jnp.maximum(one_minus_a2, 0.0)) * gi * xc
    for j in range(NSEG):
        src = slice(j * SEG, (j + 1) * SEG)
        dst = slice(j * LRU_PITCH, j * LRU_PITCH + SEG)
        for q in range(LRU_LT):
            a_ref[q, dst, :] = a[src, q * LANE:(q + 1) * LANE]
            b_ref[q, dst, :] = bt[src, q * LANE:(q + 1) * LANE]

    def local_scan(k, carry):
        rows = pl.ds(k, NSEG, stride=LRU_PITCH)
        out = []
        for q in range(LRU_LT):
            h, ap = carry[q]
            aq = a_ref[q, rows, :]
            h = aq * h + b_ref[q, rows, :]
            ap = ap * aq
            b_ref[q, rows, :] = h
            a_ref[q, rows, :] = ap
            out.append((h, ap))
        return tuple(out)

    init = (jnp.zeros((NSEG, LANE), F32), jnp.ones((NSEG, LANE), F32))
    ends = lax.fori_loop(0, SEG, local_scan, tuple(init for _ in range(LRU_LT)))
    e = jnp.concatenate([ends[q][0] for q in range(LRU_LT)], axis=1)
    ap = jnp.concatenate([ends[q][1] for q in range(LRU_LT)], axis=1)

    s = hc_ref[...]
    g = gb_ref[...]
    for j in range(NSEG):
        rs = slice(j * SEG, (j + 1) * SEG)
        ps = slice(j * LRU_PITCH, j * LRU_PITCH + SEG)
        hl = jnp.concatenate([b_ref[q, ps, :] for q in range(LRU_LT)], axis=1)
        al = jnp.concatenate([a_ref[q, ps, :] for q in range(LRU_LT)], axis=1)
        h = hl + al * s
        o = h * jax.nn.gelu(g[rs, :])
        o = o * lax.rsqrt(jnp.mean(o * o, axis=-1, keepdims=True) + EPS) * ng_ref[...]
        o_ref[rs, :] = o.astype(BF16)
        s = e[j:j + 1] + ap[j:j + 1] * s
    hc_ref[...] = s


def _lru_mixer(proj, p, layer, bsz, seq):
    t = bsz * seq
    nb = seq // TS
    res = functools.partial(_resident, layer=layer)
    vec = res((1, LRU_WIDTH))
    return pl.pallas_call(
        _lru_kernel,
        grid=(bsz, nb),
        in_specs=[
            pl.BlockSpec((TS, LRU_WIDTH), lambda b, s: (b * nb + s, COL_LRU_X // LRU_WIDTH)),
            pl.BlockSpec((TS, LRU_WIDTH), lambda b, s: (b * nb + s, COL_LRU_G // LRU_WIDTH)),
            res((CONV_WIDTH, LRU_WIDTH)), vec,
            res((LRU_WIDTH, LRU_WIDTH)), vec,
            res((LRU_WIDTH, LRU_WIDTH)), vec,
            vec, vec,
        ],
        out_specs=pl.BlockSpec((TS, LRU_WIDTH), lambda b, s: (b * nb + s, 0)),
        out_shape=jax.ShapeDtypeStruct((t, LRU_WIDTH), BF16),
        scratch_shapes=[
            pltpu.VMEM((TS + SUBLANE, LRU_WIDTH), F32),
            pltpu.VMEM((LRU_LT, NSEG * LRU_PITCH, LANE), F32),
            pltpu.VMEM((LRU_LT, NSEG * LRU_PITCH, LANE), F32),
            pltpu.VMEM((1, LRU_WIDTH), F32),
        ],
        compiler_params=_cparams(("arbitrary", "arbitrary")),
        name="rglru_mixer",
    )(proj, proj, p["conv_w"], p["conv_b"], p["wa"], p["ba"], p["wx"], p["bx"], p["lam"], p["norm_g"])


def _lru_params(conv_w, conv_b, w_a, b_a, w_x, b_x, lam, norm_g):
    eye = jnp.eye(LRU_HEADS, dtype=F32)
    bd = lambda m: (m[:, :, None, :] * eye[:, None, :, None]).reshape(LRU_WIDTH, LRU_WIDTH).astype(BF16)
    row = lambda v: v.reshape(1, LRU_WIDTH)
    return {"conv_w": conv_w, "conv_b": row(conv_b), "wa": bd(w_a), "ba": row(b_a),
            "wx": bd(w_x), "bx": row(b_x), "lam": row(lam), "norm_g": row(norm_g)}


def _split3(x):
    hi = x.astype(BF16)
    r1 = x - hi.astype(F32)
    mid = r1.astype(BF16)
    lo = (r1 - mid.astype(F32)).astype(BF16)
    return hi, mid, lo


def _gla_kernel(q_ref, k_ref, v_ref, og_ref, gl_ref, wg_ref, bg_ref, hg_ref, o_ref, st_ref):
    @pl.when(pl.program_id(1) == 0)
    def _():
        st_ref[...] = jnp.zeros_like(st_ref)

    c = GLA_CHUNK
    hc = GLA_HEADS * c
    ri = lax.broadcasted_iota(jnp.int32, (c, c), 0)
    ci = lax.broadcasted_iota(jnp.int32, (c, c), 1)
    tri = (ri >= ci).astype(BF16)
    rr = lax.broadcasted_iota(jnp.int32, (hc, hc), 0)
    cc = lax.broadcasted_iota(jnp.int32, (hc, hc), 1)
    causal = jnp.logical_and(rr // c == cc // c, rr % c >= cc % c)
    heads = lambda m, w: jnp.concatenate([m[:, h * w:(h + 1) * w] for h in range(GLA_HEADS)], axis=0)

    for n in range(TS // c):
        rs = slice(n * c, (n + 1) * c)
        logit = _dot(gl_ref[rs, :].astype(BF16), wg_ref[...]) + bg_ref[...]
        la = jax.nn.log_sigmoid(logit) / GLA_TAU
        hi, mid, lo = _split3(la)
        bcum = _dot(tri, hi) + _dot(tri, mid) + _dot(tri, lo)
        b_last = bcum[c - 1:c, :]
        kk = k_ref[rs, :]
        q_dec = q_ref[rs, :] * (GLA_DK ** -0.5) * jnp.exp(bcum)
        k_in = kk * jnp.exp(-bcum)
        k_out = (kk * jnp.exp(b_last - bcum)).astype(BF16)
        gamma = jnp.exp(b_last)
        vb = v_ref[rs, :].astype(BF16)
        qd = q_dec.astype(BF16)

        att = lax.dot_general(heads(qd, GLA_DKP), heads(k_in.astype(BF16), GLA_DKP),
                              (((1,), (1,)), ((), ())), preferred_element_type=F32)
        att = jnp.where(causal, att, 0.0).astype(BF16)
        o_s = _dot(att, heads(vb, GLA_DVP))
        outs = []
        for h in range(GLA_HEADS):
            ks = slice(h * GLA_DKP, (h + 1) * GLA_DKP)
            vs = slice(h * GLA_DVP, (h + 1) * GLA_DVP)
            st = st_ref[h]
            o_h = o_s[h * c:(h + 1) * c, :] + lax.dot_general(
                qd[:, ks], st.astype(BF16), (((1,), (1,)), ((), ())), preferred_element_type=F32)
            delta = lax.dot_general(vb[:, vs], k_out[:, ks], (((0,), (0,)), ((), ())),
                                    preferred_element_type=F32)
            st_ref[h] = gamma[:, ks] * st + delta
            ms = jnp.sum(o_h * o_h, axis=-1, keepdims=True) * (1.0 / GLA_DV)
            outs.append(o_h * lax.rsqrt(ms + EPS) * hg_ref[...])
        o = jnp.concatenate(outs, axis=1)
        og = og_ref[rs, :]
        o_ref[rs, :] = (o * (og * jax.nn.sigmoid(og))).astype(BF16)


def _gla_mixer(proj, p, layer, bsz, seq):
    t = bsz * seq
    nb = seq // TS
    row = lambda b, s: b * nb + s
    res = functools.partial(_resident, layer=layer)
    return pl.pallas_call(
        _gla_kernel,
        grid=(bsz, nb),
        in_specs=[
            pl.BlockSpec((TS, GLA_QW), lambda b, s: (row(b, s), COL_Q // GLA_QW)),
            pl.BlockSpec((TS, GLA_QW), lambda b, s: (row(b, s), COL_K // GLA_QW)),
            pl.BlockSpec((TS, GLA_VW), lambda b, s: (row(b, s), COL_V // GLA_VW)),
            pl.BlockSpec((TS, GLA_VW), lambda b, s: (row(b, s), COL_OG // GLA_VW)),
            pl.BlockSpec((TS, LANE), lambda b, s: (row(b, s), COL_GLOW // LANE)),
            res((LANE, GLA_QW)),
            res((1, GLA_QW)),
            res((1, GLA_DVP)),
        ],
        out_specs=pl.BlockSpec((TS, GLA_VW), lambda b, s: (row(b, s), 0)),
        out_shape=jax.ShapeDtypeStruct((t, GLA_VW), BF16),
        scratch_shapes=[pltpu.VMEM((GLA_HEADS, GLA_DVP, GLA_DKP), F32)],
        compiler_params=_cparams(("arbitrary", "arbitrary")),
        name="gla_mixer",
    )(proj, proj, proj, proj, proj, p["w_g2"], p["b_g2"], p["head_g"])


def _pad_heads(m, d, dp):
    lead = m.shape[:-1]
    m = m.reshape(lead + (GLA_HEADS, d))
    m = jnp.pad(m, [(0, 0)] * len(lead) + [(0, 0), (0, dp - d)])
    return m.reshape(lead + (GLA_HEADS * dp,))


def _gla_params(w_g2, b_g2, head_g):
    w = jnp.pad(_pad_heads(w_g2, GLA_DK, GLA_DKP), ((0, LANE - GLA_GATE_RANK), (0, 0)))
    return {"w_g2": w.astype(BF16),
            "b_g2": _pad_heads(b_g2, GLA_DK, GLA_DKP).reshape(1, GLA_QW),
            "head_g": jnp.pad(head_g, (0, GLA_DVP - GLA_DV)).reshape(1, GLA_DVP)}


def _pack_w_in(w):
    s5 = w[:, 0:512]
    lx = w[:, 512:1280]
    lg = w[:, 1280:2048]
    q = w[:, 2048:2432]
    k = w[:, 2432:2816]
    v = w[:, 2816:3584]
    gl = w[:, 3584:3600]
    og = w[:, 3600:4368]
    cols = [lx, lg, s5, _pad_heads(q, GLA_DK, GLA_DKP), _pad_heads(k, GLA_DK, GLA_DKP),
            _pad_heads(v, GLA_DV, GLA_DVP), _pad_heads(og, GLA_DV, GLA_DVP),
            jnp.pad(gl, ((0, 0), (0, LANE - GLA_GATE_RANK)))]
    return jnp.concatenate(cols, axis=1).astype(BF16)


def _pack_w_out(w):
    gla = w[S5_WIDTH + LRU_WIDTH:].reshape(GLA_HEADS, GLA_DV, D_MODEL)
    gla = jnp.pad(gla, ((0, 0), (0, GLA_DVP - GLA_DV), (0, 0))).reshape(GLA_VW, D_MODEL)
    return jnp.concatenate([w[:S5_WIDTH + LRU_WIDTH], gla], axis=0).astype(BF16)


def _outproj_kernel(ys_ref, yl_ref, yg_ref, x_ref, w_ref, gate_ref, g_ref, sh_ref, sc_ref, *rest,
                    with_router):
    r0, r1 = S5_WIDTH, S5_WIDTH + LRU_WIDTH
    mix = (_dot(ys_ref[...], w_ref[0:r0, :]) + _dot(yl_ref[...], w_ref[r0:r1, :])
           + _dot(yg_ref[...], w_ref[r1:NP_OUT_ROWS, :]))
    x1 = x_ref[...] + gate_ref[...] * mix
    h = _modulated_norm(x1, g_ref[...], sc_ref[...], sh_ref[...])
    if with_router:
        wr_hi_ref, wr_lo_ref, x1_ref, h_ref, lg_ref = rest
        h_ref[...] = h
        hh = h.astype(BF16)
        hm = (h - hh.astype(F32)).astype(BF16)
        lg_ref[...] = (_dot(hh, wr_hi_ref[...]) + _dot(hm, wr_hi_ref[...])
                       + _dot(hh, wr_lo_ref[...]))
    else:
        x1_ref, h_ref = rest
        h_ref[...] = h.astype(BF16)
    x1_ref[...] = x1


def _out_projection(ys, yl, yg, x2, w_packed, g, mod3, layer, bsz, seq, router=None):
    t = x2.shape[0]
    tpb = seq // TM_PROJ
    rowblk = lambda w: pl.BlockSpec((TM_PROJ, w), lambda i: (i, 0))
    in_specs = [rowblk(S5_WIDTH), rowblk(LRU_WIDTH), rowblk(GLA_VW), rowblk(D_MODEL),
                _resident((NP_OUT_ROWS, D_MODEL), layer),
                _mod_spec(layer, bsz, 2, tpb),
                _resident((1, D_MODEL), layer),
                _mod_spec(layer, bsz, 3, tpb), _mod_spec(layer, bsz, 4, tpb)]
    args = [ys, yl, yg, x2, w_packed, mod3, g, mod3, mod3]
    if router is None:
        out_specs = [rowblk(D_MODEL), rowblk(D_MODEL)]
        out_shape = [jax.ShapeDtypeStruct((t, D_MODEL), F32), jax.ShapeDtypeStruct((t, D_MODEL), BF16)]
    else:
        wr = jnp.pad(router, ((0, 0), (0, LANE - N_EXPERTS)))
        wr_hi = wr.astype(BF16)
        wr_lo = (wr - wr_hi.astype(F32)).astype(BF16)
        in_specs += [_resident((D_MODEL, LANE)), _resident((D_MODEL, LANE))]
        args += [wr_hi, wr_lo]
        out_specs = [rowblk(D_MODEL), rowblk(D_MODEL), rowblk(LANE)]
        out_shape = [jax.ShapeDtypeStruct((t, D_MODEL), F32), jax.ShapeDtypeStruct((t, D_MODEL), F32),
                     jax.ShapeDtypeStruct((t, LANE), F32)]
    return pl.pallas_call(
        functools.partial(_outproj_kernel, with_router=router is not None),
        grid=(t // TM_PROJ,),
        in_specs=in_specs, out_specs=out_specs, out_shape=out_shape,
        compiler_params=_cparams(("arbitrary",)),
        name="out_projection",
    )(*args)


def _ffn_kernel(se_ref, sn_ref, sx_ref, x_ref, w1_ref, w3_ref, w2_ref, o_ref, w1b_ref, w3b_ref, w2b_ref,
                *, q_tiles):
    s = pl.program_id(0)
    n = sn_ref[s]

    @pl.when(pl.program_id(1) == 0)
    def _():
        o_ref[...] = jnp.zeros_like(o_ref)

    @pl.when(n > 0)
    def _():
        w1b_ref[...] = w1_ref[...].astype(BF16)
        w3b_ref[...] = w3_ref[...].astype(BF16)
        w2b_ref[...] = w2_ref[...].astype(BF16)

    def swiglu_rows(first, count):
        rows = slice(first * FFN_TM, (first + count) * FFN_TM)
        xq = x_ref[rows, :]
        h1 = _dot(xq, w1b_ref[...])
        h3 = _dot(xq, w3b_ref[...])
        a = (h1 * jax.nn.sigmoid(h1) * h3).astype(BF16)
        o_ref[rows, :] += _dot(a, w2b_ref[...])

    for g0 in range(0, q_tiles, FFN_GROUP):
        gsz = min(FFN_GROUP, q_tiles - g0)
        if gsz > 1:
            pl.when(n >= g0 + gsz)(functools.partial(swiglu_rows, g0, gsz))
        for q in range(g0, g0 + gsz):
            cond = jnp.logical_and(q < n, n < g0 + gsz) if gsz > 1 else q < n
            pl.when(cond)(functools.partial(swiglu_rows, q, 1))


def _ffn(tables, xs, w1, w3, w2, q_tiles):
    r, d = xs.shape
    dff = w1.shape[2]
    nf = dff // FFN_FC
    rows = q_tiles * FFN_TM
    ns = r // rows

    def fidx(f, s, sn):
        return jnp.where(sn[s] > 0, f, nf - 1)

    grid_spec = pltpu.PrefetchScalarGridSpec(
        num_scalar_prefetch=3,
        grid=(ns, nf),
        in_specs=[
            pl.BlockSpec((rows, d), lambda s, f, se, sn, sx: (sx[s], 0), pipeline_mode=pl.Buffered(1)),
            pl.BlockSpec((None, d, FFN_FC), lambda s, f, se, sn, sx: (se[s], 0, fidx(f, s, sn))),
            pl.BlockSpec((None, d, FFN_FC), lambda s, f, se, sn, sx: (se[s], 0, fidx(f, s, sn))),
            pl.BlockSpec((None, FFN_FC, d), lambda s, f, se, sn, sx: (se[s], fidx(f, s, sn), 0)),
        ],
        out_specs=pl.BlockSpec((rows, d), lambda s, f, se, sn, sx: (s, 0),
                               pipeline_mode=pl.Buffered(1)),
        scratch_shapes=[pltpu.VMEM((d, FFN_FC), BF16), pltpu.VMEM((d, FFN_FC), BF16),
                        pltpu.VMEM((FFN_FC, d), BF16)],
    )
    return pl.pallas_call(
        functools.partial(_ffn_kernel, q_tiles=q_tiles),
        grid_spec=grid_spec,
        out_shape=jax.ShapeDtypeStruct((r, d), F32),
        compiler_params=_cparams(("arbitrary", "arbitrary")),
        name="ffn_swiglu",
    )(*tables, xs, w1, w3, w2)


def _dense_tables(t):
    ns = t // (FFN_Q_DENSE * FFN_TM)
    return (jnp.zeros((ns,), jnp.int32), jnp.full((ns,), FFN_Q_DENSE, jnp.int32),
            jnp.arange(ns, dtype=jnp.int32))


def _route_kernel(lg_ref, o_ref, cnt_ref, carry_ref):
    @pl.when(pl.program_id(0) == 0)
    def _():
        carry_ref[...] = jnp.zeros_like(carry_ref)

    tm = TM_ROUTE
    lane = lax.broadcasted_iota(jnp.int32, (tm, LANE), 1)
    neg = jnp.float32(-jnp.inf)
    l = jnp.where(lane < N_EXPERTS, lg_ref[...], neg)
    m1 = jnp.max(l, axis=1, keepdims=True)
    i1 = jnp.min(jnp.where(l == m1, lane, LANE), axis=1, keepdims=True)
    oh1 = lane == i1
    l2 = jnp.where(oh1, neg, l)
    m2 = jnp.max(l2, axis=1, keepdims=True)
    i2 = jnp.min(jnp.where(l2 == m2, lane, LANE), axis=1, keepdims=True)
    oh2 = lane == i2
    e = jnp.exp(m2 - m1)
    den = 1.0 + e
    g1 = 1.0 / den
    g2 = e / den
    oh = jnp.where(jnp.logical_or(oh1, oh2), 1.0, 0.0)
    ri = lax.broadcasted_iota(jnp.int32, (tm, tm), 0)
    ci = lax.broadcasted_iota(jnp.int32, (tm, tm), 1)
    before = (ri > ci).astype(BF16)
    cum = _dot(before, oh.astype(BF16)) + carry_ref[...]
    r1 = jnp.sum(jnp.where(oh1, cum, 0.0), axis=1, keepdims=True)
    r2 = jnp.sum(jnp.where(oh2, cum, 0.0), axis=1, keepdims=True)
    carry_ref[...] = carry_ref[...] + jnp.sum(oh, axis=0, keepdims=True)
    cnt_ref[...] = carry_ref[...]
    out = jnp.where(lane == 0, i1.astype(F32), 0.0)
    out = jnp.where(lane == 1, i2.astype(F32), out)
    out = jnp.where(lane == 2, g1, out)
    out = jnp.where(lane == 3, g2, out)
    out = jnp.where(lane == 4, r1, out)
    out = jnp.where(lane == 5, r2, out)
    o_ref[...] = out


def _route(logits):
    t = logits.shape[0]
    return pl.pallas_call(
        _route_kernel,
        grid=(t // TM_ROUTE,),
        in_specs=[pl.BlockSpec((TM_ROUTE, LANE), lambda i: (i, 0))],
        out_specs=[pl.BlockSpec((TM_ROUTE, LANE), lambda i: (i, 0)),
                   pl.BlockSpec((1, LANE), lambda i: (0, 0))],
        out_shape=[jax.ShapeDtypeStruct((t, LANE), F32), jax.ShapeDtypeStruct((1, LANE), F32)],
        scratch_shapes=[pltpu.VMEM((1, LANE), F32)],
        compiler_params=_cparams(("arbitrary",)),
        name="route_top2",
    )(logits)


def _expert_tables(route, counts, t):
    qt = FFN_Q_EXPERT
    rows = qt * FFN_TM
    max_tiles = (2 * t + FFN_TM - 1) // FFN_TM + N_EXPERTS
    ns = (max_tiles + N_EXPERTS * (qt - 1)) // qt
    i1 = route[:, 0].astype(jnp.int32)
    i2 = route[:, 1].astype(jnp.int32)
    r1 = route[:, 4].astype(jnp.int32)
    r2 = route[:, 5].astype(jnp.int32)
    cnt = counts[0, :N_EXPERTS].astype(jnp.int32)
    tiles = (cnt + FFN_TM - 1) // FFN_TM
    supers = (tiles + qt - 1) // qt
    super_end = jnp.cumsum(supers)
    super_start = super_end - supers
    n_used = super_end[-1]
    ids = jnp.arange(ns, dtype=jnp.int32)
    sx = jnp.minimum(ids, n_used - 1)
    se = jnp.minimum(jnp.sum((sx[:, None] >= super_end[None, :]).astype(jnp.int32), axis=1),
                     N_EXPERTS - 1)
    sn = jnp.clip(tiles[se] - (sx - super_start[se]) * qt, 0, qt)
    sn = jnp.where(ids < n_used, sn, 0).astype(jnp.int32)
    row_off = super_start * rows
    pos1 = row_off[i1] + r1
    pos2 = row_off[i2] + r2
    tok = jnp.arange(t, dtype=jnp.int32)
    src = jnp.zeros((ns * rows,), jnp.int32).at[pos1].set(tok).at[pos2].set(tok)
    gpt = rows // TM_GATHER
    gid = jnp.arange(ns * gpt, dtype=jnp.int32)
    gv = ((gid % gpt) * TM_GATHER < sn[gid // gpt] * FFN_TM).astype(jnp.int32)
    return (se, sn, sx), pos1, pos2, src, gv


def _row_copy(src_hbm, row, dst, slot, sem):
    return pltpu.make_async_copy(src_hbm.at[pl.ds(row, 1), :], dst.at[pl.ds(slot, 1), :], sem)


def _tile_copy(src_hbm, dst, sem):
    return pltpu.make_async_copy(src_hbm.at[pl.ds(0, dst.shape[0]), :], dst, sem)


def _gather_kernel(src_ref, gv_ref, h_hbm, o_ref, buf_ref, sem):
    i = pl.program_id(0)
    slot = i % 2

    def issue(tile, sl):
        base = tile * TM_GATHER

        def body(r, c):
            for p in range(2):
                row = 2 * r + p
                _row_copy(h_hbm, src_ref[base + row], buf_ref.at[sl], row, sem.at[sl]).start(priority=p)
            return c

        lax.fori_loop(0, TM_GATHER // 2, body, 0, unroll=4)

    @pl.when(jnp.logical_and(i == 0, gv_ref[0] == 1))
    def _():
        issue(0, 0)

    nxt = jnp.minimum(i + 1, pl.num_programs(0) - 1)

    @pl.when(jnp.logical_and(i + 1 < pl.num_programs(0), gv_ref[nxt] == 1))
    def _():
        issue(i + 1, 1 - slot)

    @pl.when(gv_ref[i] == 1)
    def _():
        _tile_copy(h_hbm, buf_ref.at[slot], sem.at[slot]).wait()
        o_ref[...] = buf_ref[slot].astype(BF16)

    @pl.when(gv_ref[i] == 0)
    def _():
        o_ref[...] = jnp.zeros_like(o_ref)


def _gather_rows(src, gv, h):
    nt = gv.shape[0]
    d = h.shape[1]
    grid_spec = pltpu.PrefetchScalarGridSpec(
        num_scalar_prefetch=2,
        grid=(nt,),
        in_specs=[pl.BlockSpec(memory_space=pl.ANY)],
        out_specs=pl.BlockSpec((TM_GATHER, d), lambda i, src, gv: (i, 0)),
        scratch_shapes=[pltpu.VMEM((2, TM_GATHER, d), F32), pltpu.SemaphoreType.DMA((2,))],
    )
    return pl.pallas_call(
        _gather_kernel,
        grid_spec=grid_spec,
        out_shape=jax.ShapeDtypeStruct((nt * TM_GATHER, d), BF16),
        compiler_params=_cparams(("arbitrary",)),
        name="expert_gather",
    )(src, gv, h)


def _combine_kernel(p1_ref, p2_ref, y_hbm, x1_ref, gate_ref, rt_ref, fg_ref, o_ref, ya_ref, yb_ref, sem):
    base = pl.program_id(0) * TM_COMB

    def issue(r, c):
        _row_copy(y_hbm, p1_ref[base + r], ya_ref, r, sem.at[0]).start(priority=0)
        _row_copy(y_hbm, p2_ref[base + r], yb_ref, r, sem.at[1]).start(priority=1)
        return c

    lax.fori_loop(0, TM_COMB, issue, 0, unroll=8)
    _tile_copy(y_hbm, ya_ref, sem.at[0]).wait()
    _tile_copy(y_hbm, yb_ref, sem.at[1]).wait()
    rt = rt_ref[...]
    f = rt[:, 2:3] * ya_ref[...] + rt[:, 3:4] * yb_ref[...]
    x2 = x1_ref[...] + gate_ref[...] * f
    o_ref[...] = x2 * lax.rsqrt(jnp.mean(x2 * x2, axis=-1, keepdims=True) + EPS) * fg_ref[...]


def _combine(pos1, pos2, y, x1, mod3, route, final_g, layer, bsz, seq):
    t, d = x1.shape
    tpb = seq // TM_COMB
    grid_spec = pltpu.PrefetchScalarGridSpec(
        num_scalar_prefetch=2,
        grid=(t // TM_COMB,),
        in_specs=[
            pl.BlockSpec(memory_space=pl.ANY),
            pl.BlockSpec((TM_COMB, d), lambda i, p1, p2: (i, 0)),
            pl.BlockSpec((None, 1, d), lambda i, p1, p2: ((layer * bsz + i // tpb) * N_ADA + 5, 0, 0)),
            pl.BlockSpec((TM_COMB, LANE), lambda i, p1, p2: (i, 0)),
            pl.BlockSpec((1, d), lambda i, p1, p2: (0, 0)),
        ],
        out_specs=pl.BlockSpec((TM_COMB, d), lambda i, p1, p2: (i, 0)),
        scratch_shapes=[pltpu.VMEM((TM_COMB, d), F32), pltpu.VMEM((TM_COMB, d), F32),
                        pltpu.SemaphoreType.DMA((2,))],
    )
    return pl.pallas_call(
        _combine_kernel,
        grid_spec=grid_spec,
        out_shape=jax.ShapeDtypeStruct((t, d), F32),
        compiler_params=_cparams(("arbitrary",)),
        name="expert_combine_norm",
    )(pos1, pos2, y, x1, mod3, route, final_g.reshape(1, d))


def kernel(x, c, ada_w, ada_b, norm_mix_g, norm_ffn_g, w_in, s5_a_re, s5_a_im, s5_log_dt, s5_b_re, s5_b_im, s5_c_re, s5_c_im, s5_d, s5_w_glu, s5_b_glu, s5_norm_g, lru_conv_w, lru_conv_b, lru_w_a, lru_b_a, lru_w_x, lru_b_x, lru_lambda, lru_norm_g, gla_w_g2, gla_b_g2, gla_head_g, w_out, ffn_w1, ffn_w3, ffn_w2, moe_router, moe_w1, moe_w3, moe_w2, final_norm_g):
    bsz, seq, d = x.shape
    depth = ada_w.shape[0]
    t = bsz * seq
    assert depth == 2 and d == D_MODEL and seq % TS == 0 and t % (FFN_Q_DENSE * FFN_TM) == 0
    mod3 = _adaln_mod(c, ada_w, ada_b).reshape(depth * bsz * N_ADA, 1, D_MODEL)
    s5p = jax.vmap(_s5_params)(s5_a_re, s5_a_im, s5_log_dt, s5_b_re, s5_b_im, s5_c_re, s5_c_im, s5_d,
                               s5_w_glu, s5_b_glu, s5_norm_g)
    lrup = jax.vmap(_lru_params)(lru_conv_w, lru_conv_b, lru_w_a, lru_b_a, lru_w_x, lru_b_x,
                                 lru_lambda, lru_norm_g)
    glap = jax.vmap(_gla_params)(gla_w_g2, gla_b_g2, gla_head_g)
    w_in_p = jax.vmap(_pack_w_in)(w_in)
    w_out_p = jax.vmap(_pack_w_out)(w_out)
    g_mix = norm_mix_g.reshape(depth, 1, D_MODEL)
    g_ffn = norm_ffn_g.reshape(depth, 1, D_MODEL)

    xf = x.reshape(t, D_MODEL)
    pending = None
    out = None
    for l in range(depth):
        if pending is None:
            proj = _in_projection(xf, g_mix, mod3, w_in_p, l, bsz, seq)
        else:
            proj, xf = _in_projection(xf, g_mix, mod3, w_in_p, l, bsz, seq, residual=pending)
        y_s5 = _s5_mixer(proj, s5p, l, bsz, seq)
        y_lru = _lru_mixer(proj, lrup, l, bsz, seq)
        y_gla = _gla_mixer(proj, glap, l, bsz, seq)
        if l % 2 == 0:
            xf, h = _out_projection(y_s5, y_lru, y_gla, xf, w_out_p, g_ffn, mod3, l, bsz, seq)
            k = l // 2
            pending = _ffn(_dense_tables(t), h, ffn_w1[k:k + 1], ffn_w3[k:k + 1], ffn_w2[k:k + 1],
                           FFN_Q_DENSE)
        else:
            x1, h, logits = _out_projection(y_s5, y_lru, y_gla, xf, w_out_p, g_ffn, mod3, l, bsz, seq,
                                            router=moe_router[l // 2])
            route, counts = _route(logits)
            tables, pos1, pos2, src, gv = _expert_tables(route, counts, t)
            xs = _gather_rows(src, gv, h)
            y = _ffn(tables, xs, moe_w1[l // 2], moe_w3[l // 2], moe_w2[l // 2], FFN_Q_EXPERT)
            out = _combine(pos1, pos2, y, x1, mod3, route, final_norm_g, l, bsz, seq)
    return out.reshape(bsz, seq, D_MODEL)
```
